```python
import math
import jax, jax.numpy as jnp
from jax import lax
import numpy as np

D_MODEL = 1024
BATCH = 4
SEQ = 8192
DEPTH = 2

BRANCH_WIDTH = D_MODEL // 2
N_BRANCHES = 3
POOL_WINDOWS = (2, 4, 8, 16)
POOL_GROUPS = len(POOL_WINDOWS)
POOL_CH = BRANCH_WIDTH // POOL_GROUPS
POOL_WIDTH = POOL_GROUPS * POOL_CH
ATTN_HEAD_DIM = 64
ATTN_HEADS = BRANCH_WIDTH // ATTN_HEAD_DIM
ATTN_GROUP = 4
ATTN_KV_HEADS = ATTN_HEADS // ATTN_GROUP
ATTN_WINDOW = 128
ATTN_WIDTH = ATTN_HEADS * ATTN_HEAD_DIM
ATTN_KV_WIDTH = ATTN_KV_HEADS * ATTN_HEAD_DIM
GDN_HEAD_DIM = 128
GDN_HEADS = BRANCH_WIDTH // GDN_HEAD_DIM
GDN_WIDTH = GDN_HEADS * GDN_HEAD_DIM
GDN_CONV = 4
GDN_CHUNK = 64
NORM_EPS = 1e-6
IN_SECTIONS = (POOL_WIDTH, POOL_WIDTH, ATTN_WIDTH, ATTN_KV_WIDTH, ATTN_KV_WIDTH, ATTN_WIDTH,
               3 * GDN_WIDTH, GDN_HEADS, GDN_HEADS, GDN_WIDTH, N_BRANCHES * D_MODEL)
N_IN = sum(IN_SECTIONS)

kernel_name = "hybrid_pool_swa_gdn_gated_merge"


def rms_norm(x, g):
    xf = x.astype(jnp.float32)
    y = xf * lax.rsqrt(jnp.mean(xf * xf, axis=-1, keepdims=True) + NORM_EPS)
    return y.astype(x.dtype) * g


def l2_norm(x):
    xf = x.astype(jnp.float32)
    return xf * lax.rsqrt(jnp.sum(xf * xf, axis=-1, keepdims=True) + NORM_EPS)


def pool_mixer(xp, pool_w, pool_scale):
    b, s, _ = xp.shape
    xg = xp.reshape(b, s, POOL_GROUPS, POOL_CH).astype(jnp.float32)
    cs = jnp.cumsum(xg, axis=1)
    pos = jnp.arange(1, s + 1, dtype=jnp.float32)[None, :, None]
    outs = []
    for gi, win in enumerate(POOL_WINDOWS):
        c = cs[:, :, gi]
        prev = jnp.pad(c, ((0, 0), (win, 0), (0, 0)))[:, :s]
        outs.append((c - prev) / jnp.minimum(pos, float(win)) - xg[:, :, gi])
    pooled = jnp.stack(outs, axis=2)
    mixed = jnp.einsum('bsgc,gcd->bsgd', pooled, pool_w.astype(jnp.float32)).reshape(b, s, POOL_WIDTH)
    return (mixed * pool_scale.astype(jnp.float32)).astype(xp.dtype)


def sliding_window_attention(q, k, v, sinks):
    b, s, hkv, grp, d = q.shape
    w = ATTN_WINDOW
    n = s // w
    qb = q.reshape(b, n, w, hkv, grp, d)
    kb = k.reshape(b, n, w, hkv, d)
    vb = v.reshape(b, n, w, hkv, d)

    def with_prev(t):
        prev = jnp.concatenate([jnp.zeros_like(t[:, :1]), t[:, :-1]], axis=1)
        return jnp.concatenate([prev, t], axis=2)

    kk, vv = with_prev(kb), with_prev(vb)
    scores = jnp.einsum('bnqhgd,bnkhd->bnhgqk', qb, kk).astype(jnp.float32) * (d ** -0.5)
    qi = jnp.arange(w)[:, None]
    kj = jnp.arange(2 * w)[None, :]
    dist = qi + w - kj
    in_window = (dist >= 0) & (dist < w)
    has_prev = (jnp.arange(n) > 0)[:, None, None] | (kj >= w)[None]
    valid = in_window[None] & has_prev
    n_heads = hkv * grp
    slopes = jnp.exp2(-8.0 * jnp.arange(1, n_heads + 1, dtype=jnp.float32) / n_heads).reshape(hkv, grp)
    scores = scores - slopes[:, :, None, None] * dist.astype(jnp.float32)
    scores = jnp.where(valid[None, :, None, None], scores, -jnp.inf)
    sink = sinks.astype(jnp.float32).reshape(hkv, grp)[:, :, None, None]
    m = jnp.maximum(jnp.max(scores, axis=-1, keepdims=True), sink)
    p = jnp.exp(scores - m)
    p = p / (jnp.sum(p, axis=-1, keepdims=True) + jnp.exp(sink - m))
    out = jnp.einsum('bnhgqk,bnkhd->bnqhgd', p.astype(v.dtype), vv)
    return out.reshape(b, s, n_heads, d)


def causal_conv(x, w):
    kw, ch = w.shape
    return lax.conv_general_dilated(x, w.reshape(kw, 1, ch).astype(x.dtype), window_strides=(1,),
                                    padding=[(kw - 1, 0)], dimension_numbers=('NWC', 'WIO', 'NWC'),
                                    feature_group_count=ch)


def gated_delta_rule(q, k, v, beta, g):
    b, s, h, dk = q.shape
    dv = v.shape[-1]
    c = GDN_CHUNK
    n = s // c

    def to_chunks(t):
        return jnp.swapaxes(t.reshape(b, n, c, h, *t.shape[3:]), 2, 3)

    q = to_chunks(q * (dk ** -0.5))
    k = to_chunks(k)
    v = to_chunks(v)
    beta = to_chunks(beta)
    g = jnp.cumsum(to_chunks(g), axis=-1)
    idx = jnp.arange(c)
    causal = idx[:, None] >= idx[None, :]
    strict = idx[:, None] > idx[None, :]
    decay = jnp.exp(jnp.where(causal, g[..., :, None] - g[..., None, :], -jnp.inf))
    kbeta = k * beta[..., None]
    a_mat = jnp.where(strict, jnp.einsum('bnhid,bnhjd->bnhij', kbeta, k) * decay, 0.0) + jnp.eye(c, dtype=q.dtype)
    u = lax.linalg.triangular_solve(a_mat, v * beta[..., None], left_side=True, lower=True, unit_diagonal=True)
    w = lax.linalg.triangular_solve(a_mat, kbeta * jnp.exp(g)[..., None], left_side=True, lower=True, unit_diagonal=True)
    qk = jnp.einsum('bnhid,bnhjd->bnhij', q, k) * decay
    q_dec = q * jnp.exp(g)[..., None]
    g_last = g[..., -1]
    k_dec = k * jnp.exp(g_last[..., None] - g)[..., None]

    def step(state, inp):
        qk_i, qd_i, w_i, u_i, kd_i, gl_i = inp
        v_new = u_i - jnp.einsum('bhcd,bhde->bhce', w_i, state)
        o_i = jnp.einsum('bhcd,bhde->bhce', qd_i, state) + jnp.einsum('bhij,bhje->bhie', qk_i, v_new)
        state = state * jnp.exp(gl_i)[..., None, None] + jnp.einsum('bhcd,bhce->bhde', kd_i, v_new)
        return state, o_i

    xs = tuple(jnp.moveaxis(t, 1, 0) for t in (qk, q_dec, w, u, k_dec, g_last))
    state0 = jnp.zeros((b, h, dk, dv), q.dtype)
    _, o = lax.scan(step, state0, xs)
    return jnp.swapaxes(jnp.moveaxis(o, 0, 1), 2, 3).reshape(b, s, h, dv)


def hybrid_layer(x, norm_g, w_in, pool_w, pool_scale, attn_sinks, conv_w, a_log, dt_bias,
                 gdn_norm_g, w_branch, w_out):
    b, s, _ = x.shape
    h = rms_norm(x, norm_g)
    proj = jnp.einsum('bsd,dn->bsn', h, w_in)
    (p_v, p_z, a_q, a_k, a_v, a_z, c_qkv, c_a, c_b, c_z, m_g) = jnp.split(
        proj, np.cumsum(IN_SECTIONS)[:-1].tolist(), axis=-1)
    br_a = pool_mixer(p_v, pool_w, pool_scale) * jax.nn.silu(p_z)
    q = a_q.reshape(b, s, ATTN_KV_HEADS, ATTN_GROUP, ATTN_HEAD_DIM)
    k = a_k.reshape(b, s, ATTN_KV_HEADS, ATTN_HEAD_DIM)
    v = a_v.reshape(b, s, ATTN_KV_HEADS, ATTN_HEAD_DIM)
    br_b = sliding_window_attention(q, k, v, attn_sinks).reshape(b, s, ATTN_WIDTH).astype(x.dtype) * jax.nn.silu(a_z)
    c_qkv = jax.nn.silu(causal_conv(c_qkv, conv_w))
    cq, ck, cv = jnp.split(c_qkv, 3, axis=-1)
    cq = l2_norm(cq.reshape(b, s, GDN_HEADS, GDN_HEAD_DIM))
    ck = l2_norm(ck.reshape(b, s, GDN_HEADS, GDN_HEAD_DIM))
    cv = cv.reshape(b, s, GDN_HEADS, GDN_HEAD_DIM).astype(jnp.float32)
    beta = jax.nn.sigmoid(c_b.astype(jnp.float32))
    log_decay = -jnp.exp(a_log.astype(jnp.float32)) * jax.nn.softplus(c_a.astype(jnp.float32) + dt_bias.astype(jnp.float32))
    o = gated_delta_rule(cq, ck, cv, beta, log_decay)
    br_c = rms_norm(o, gdn_norm_g.astype(jnp.float32)).reshape(b, s, GDN_WIDTH).astype(x.dtype) * jax.nn.silu(c_z)
    branches = jnp.stack([br_a, br_b, br_c], axis=2)
    up = jnp.einsum('bsnc,ncd->bsnd', branches, w_branch)
    gates = jax.nn.sigmoid(m_g.reshape(b, s, N_BRANCHES, D_MODEL))
    merged = jnp.sum(gates * up, axis=2)
    return x + jnp.einsum('bsd,de->bse', merged, w_out)


def setup_inputs(seed: int = 0) -> dict:
    key = jax.random.key(seed)
    ks = jax.random.split(key, 13)
    f32 = jnp.float32

    def nrm(k, shape, scale):
        return scale * jax.random.normal(k, shape, f32)

    x = jax.random.normal(ks[0], (BATCH, SEQ, D_MODEL), f32)
    norm_g = 1.0 + nrm(ks[1], (DEPTH, D_MODEL), 0.02)
    w_in = nrm(ks[2], (DEPTH, D_MODEL, N_IN), D_MODEL ** -0.5)
    pool_w = nrm(ks[3], (DEPTH, POOL_GROUPS, POOL_CH, POOL_CH), POOL_CH ** -0.5)
    pool_scale = 1.0 + nrm(ks[4], (DEPTH, POOL_WIDTH), 0.02)
    attn_sinks = nrm(ks[5], (DEPTH, ATTN_HEADS), 0.5)
    conv_w = nrm(ks[6], (DEPTH, GDN_CONV, 3 * GDN_WIDTH), GDN_CONV ** -0.5)
    a_log = jnp.log(jax.random.uniform(ks[7], (DEPTH, GDN_HEADS), f32, 1.0, 16.0))
    dt = jnp.exp(jax.random.uniform(ks[8], (DEPTH, GDN_HEADS), f32, math.log(1e-3), math.log(1e-1)))
    dt_bias = dt + jnp.log(-jnp.expm1(-dt))
    gdn_norm_g = 1.0 + nrm(ks[9], (DEPTH, GDN_HEAD_DIM), 0.02)
    w_branch = nrm(ks[10], (DEPTH, N_BRANCHES, BRANCH_WIDTH, D_MODEL), BRANCH_WIDTH ** -0.5)
    w_out = nrm(ks[11], (DEPTH, D_MODEL, D_MODEL), D_MODEL ** -0.5)
    final_norm_g = 1.0 + nrm(ks[12], (D_MODEL,), 0.02)
    return {"x": x, "norm_g": norm_g, "w_in": w_in, "pool_w": pool_w, "pool_scale": pool_scale,
            "attn_sinks": attn_sinks, "conv_w": conv_w, "a_log": a_log, "dt_bias": dt_bias,
            "gdn_norm_g": gdn_norm_g, "w_branch": w_branch, "w_out": w_out,
            "final_norm_g": final_norm_g}


def reference(x, norm_g, w_in, pool_w, pool_scale, attn_sinks, conv_w, a_log, dt_bias,
              gdn_norm_g, w_branch, w_out, final_norm_g):
    for l in range(DEPTH):
        x = hybrid_layer(x, norm_g[l], w_in[l], pool_w[l], pool_scale[l], attn_sinks[l], conv_w[l],
                         a_log[l], dt_bias[l], gdn_norm_g[l], w_branch[l], w_out[l])
    return rms_norm(x, final_norm_g)
```

```python
import functools

import jax
import jax.numpy as jnp
from jax import lax
from jax.experimental import pallas as pl
from jax.experimental.pallas import tpu as pltpu

F32 = jnp.float32
BF16 = jnp.bfloat16

D_MODEL = 1024
DEPTH = 2
BRANCH_WIDTH = 512
POOL_WINDOWS = (2, 4, 8, 16)
POOL_CH = 128
ATTN_HEAD_DIM = 64
ATTN_HEADS = 8
ATTN_WINDOW = 128
GDN_HEAD_DIM = 128
GDN_HEADS = 4
GDN_CONV = 4
GDN_CHUNK = 64
NORM_EPS = 1e-6

_COLS = dict(pv=(0, 512), pz=(512, 1024), aq=(1024, 1536), ak=(1536, 1664), av=(1664, 1792),
             az=(1792, 2304), cqkv=(2304, 3840), cab=(3840, 3848), cz=(3848, 4360), mg=(4360, 7432))
_SECTIONS = (("mg", 3072), ("cqkv", 1536), ("pv", 512), ("pz", 512), ("aq", 512), ("az", 512),
             ("cz", 512), ("ak", 128), ("av", 128), ("cab", 128))
_PACKED_WIDTH = sum(w for _, w in _SECTIONS)

LANES = 128
TILE_IN = 256
TILE = 256
POOL_TAIL = 16
CONV_TAIL = 8
VMEM_LIMIT = 56 * 1024 * 1024


def _sigmoid(v):
    return 1.0 / (1.0 + jnp.exp(-v))


def _silu(v):
    return v * _sigmoid(v)


def _mm(a, b):
    return jnp.dot(a.astype(BF16), b.astype(BF16), preferred_element_type=F32)


def _mm_tb(a, b):
    return lax.dot_general(a.astype(BF16), b.astype(BF16), (((1,), (1,)), ((), ())),
                           preferred_element_type=F32)


def _inproj_kernel(x_ref, g_ref, w_ref, *out_refs):
    x = x_ref[...]
    ms = jnp.mean(x * x, axis=-1, keepdims=True)
    h = (x * lax.rsqrt(ms + NORM_EPS) * g_ref[...]).astype(BF16)
    off = 0
    for ref, (_, width) in zip(out_refs, _SECTIONS):
        for c in range(0, width, 512):
            cw = min(512, width - c)
            ref[:, c:c + cw] = jnp.dot(h, w_ref[:, off + c:off + c + cw],
                                       preferred_element_type=F32)
        off += width


def _inproj(xt, g_row, w_packed):
    t = xt.shape[0]
    const = lambda i: (0, 0)
    return pl.pallas_call(
        _inproj_kernel,
        grid=(t // TILE_IN,),
        in_specs=[pl.BlockSpec((TILE_IN, D_MODEL), lambda i: (i, 0)),
                  pl.BlockSpec((1, D_MODEL), const),
                  pl.BlockSpec((D_MODEL, _PACKED_WIDTH), const, pipeline_mode=pl.Buffered(1))],
        out_specs=[pl.BlockSpec((TILE_IN, w), lambda i: (i, 0)) for _, w in _SECTIONS],
        out_shape=[jax.ShapeDtypeStruct((t, w), F32) for _, w in _SECTIONS],
        compiler_params=pltpu.CompilerParams(dimension_semantics=("arbitrary",),
                                             vmem_limit_bytes=VMEM_LIMIT),
        name="in_proj",
    )(xt, g_row, w_packed)


def _pool_branch(s_idx, pv_ref, pz_ref, poolw_ref, pscale_ref, pv_tail, br_ref):
    pos = (s_idx * TILE + 1 + lax.broadcasted_iota(jnp.int32, (TILE, LANES), 0)).astype(F32)
    for gi, win in enumerate(POOL_WINDOWS):
        sl = slice(gi * POOL_CH, (gi + 1) * POOL_CH)
        cur = pv_ref[:, sl]
        acc = jnp.concatenate([pv_tail[:, sl], cur], axis=0)
        k = 1
        while k < win:
            acc = acc + pltpu.roll(acc, k, 0)
            k *= 2
        pooled = acc[POOL_TAIL:] / jnp.minimum(pos, float(win)) - cur
        mixed = _mm(pooled, poolw_ref[gi]) * pscale_ref[:, sl]
        br_ref[0, :, sl] = (mixed * _silu(pz_ref[:, sl])).astype(BF16)
        pv_tail[:, sl] = cur[TILE - POOL_TAIL:]


def _attn_branch(s_idx, aq_ref, ak_ref, av_ref, az_ref, sinks_ref, kprev, vprev, br_ref):
    w = ATTN_WINDOW
    k_all = jnp.concatenate([kprev[...], ak_ref[...]], axis=0)
    v_all = jnp.concatenate([vprev[...], av_ref[...]], axis=0)
    kprev[...] = ak_ref[TILE - w:, :]
    vprev[...] = av_ref[TILE - w:, :]
    k_sw = pltpu.roll(k_all, ATTN_HEAD_DIM, 1).astype(BF16)
    v_sw = pltpu.roll(v_all, ATTN_HEAD_DIM, 1).astype(BF16)
    k_all = k_all.astype(BF16)
    v_all = v_all.astype(BF16)

    qi = lax.broadcasted_iota(jnp.int32, (w, 2 * w), 0)
    kj = lax.broadcasted_iota(jnp.int32, (w, 2 * w), 1)
    dist = qi + w - kj
    in_window = (dist >= 0) & (dist < w)
    distf = dist.astype(F32)
    lane_lo = lax.broadcasted_iota(jnp.int32, (w, LANES), 1) < ATTN_HEAD_DIM
    scale = ATTN_HEAD_DIM ** -0.5

    for blk in range(TILE // w):
        rows = slice(blk * w, (blk + 1) * w)
        keys = slice(blk * w, blk * w + 2 * w)
        valid = in_window
        if blk == 0:
            valid = valid & ((kj >= w) | (s_idx > 0))
        for pair in range(ATTN_HEADS // 2):
            cols = slice(pair * LANES, (pair + 1) * LANES)
            q_pair = aq_ref[rows, cols]
            kv_head = pair // 2
            res = []
            for half in range(2):
                head = 2 * pair + half
                q_m = jnp.where(lane_lo if half == 0 else ~lane_lo, q_pair, 0.0)
                same = kv_head == half
                k_op = (k_all if same else k_sw)[keys]
                v_op = (v_all if same else v_sw)[keys]
                slope = 2.0 ** (-(head + 1))
                sc = _mm_tb(q_m, k_op) * scale - slope * distf
                sc = jnp.where(valid, sc, -jnp.inf)
                sink = sinks_ref[head]
                m = jnp.maximum(jnp.max(sc, axis=-1, keepdims=True), sink)
                p = jnp.exp(sc - m)
                denom = jnp.sum(p, axis=-1, keepdims=True) + jnp.exp(sink - m)
                res.append(_mm(p, v_op) / denom)
            o_pair = jnp.where(lane_lo, res[0], res[1])
            br_ref[1, rows, cols] = (o_pair * _silu(az_ref[rows, cols])).astype(BF16)


def _conv_silu(cqkv_ref, cq_tail, convw_ref, slab):
    sl = slice(slab * LANES, (slab + 1) * LANES)
    cur = cqkv_ref[:, sl]
    ext = jnp.concatenate([cq_tail[:, sl], cur], axis=0)
    acc = cur * convw_ref[GDN_CONV - 1:GDN_CONV, sl]
    for back in range(1, GDN_CONV):
        tap = GDN_CONV - 1 - back
        acc = acc + pltpu.roll(ext, back, 0)[CONV_TAIL:] * convw_ref[tap:tap + 1, sl]
    return _silu(acc)


def _l2_norm(v):
    return v * lax.rsqrt(jnp.sum(v * v, axis=-1, keepdims=True) + NORM_EPS)


def _gdn_branch(cqkv_ref, cab_ref, cz_ref, convw_ref, alog_ref, dtb_ref, gng_ref, cq_tail, state, br_ref):
    c = GDN_CHUNK
    n_chunks = TILE // c
    cab = cab_ref[...]
    sp_in = cab + dtb_ref[...]
    softplus = jnp.maximum(sp_in, 0.0) + jnp.log(1.0 + jnp.exp(-jnp.abs(sp_in)))
    g_all = -jnp.exp(alog_ref[...]) * softplus
    beta_all = _sigmoid(cab)
    row_in_chunk = lax.broadcasted_iota(jnp.int32, (TILE, LANES), 0) & (c - 1)
    gc_all = g_all
    k = 1
    while k < c:
        gc_all = gc_all + jnp.where(row_in_chunk >= k, pltpu.roll(gc_all, k, 0), 0.0)
        k *= 2
    gc_t = gc_all.T
    gl_all = jnp.broadcast_to(gc_all.reshape(n_chunks, c, LANES)[:, c - 1:c, :],
                              (n_chunks, c, LANES)).reshape(TILE, LANES)
    eg_all = jnp.exp(gc_all)
    ekd_all = jnp.exp(gl_all - gc_all)
    egl_all = jnp.exp(gl_all)

    ri = lax.broadcasted_iota(jnp.int32, (TILE, TILE), 0)
    ci = lax.broadcasted_iota(jnp.int32, (TILE, TILE), 1)
    same_chunk = (ri & -c) == (ci & -c)
    causal = same_chunk & (ri >= ci)
    strict = same_chunk & (ri > ci)
    eye = (ri == ci).astype(F32)
    lane_lo = lax.broadcasted_iota(jnp.int32, (LANES, LANES), 1) < c
    zeros_c = jnp.zeros((c, LANES), F32)
    scale = GDN_HEAD_DIM ** -0.5

    for h in range(GDN_HEADS):
        hs = slice(h * LANES, (h + 1) * LANES)
        qn = _l2_norm(_conv_silu(cqkv_ref, cq_tail, convw_ref, h)) * scale
        kn = _l2_norm(_conv_silu(cqkv_ref, cq_tail, convw_ref, GDN_HEADS + h))
        v = _conv_silu(cqkv_ref, cq_tail, convw_ref, 2 * GDN_HEADS + h)
        beta = beta_all[:, GDN_HEADS + h:GDN_HEADS + h + 1]
        gcol = gc_all[:, h:h + 1]
        eg = eg_all[:, h:h + 1]
        kb = kn * beta
        prod = _mm_tb(jnp.concatenate([kb, qn], axis=0), kn)
        decay = jnp.exp(jnp.where(causal, gcol - gc_t[h:h + 1, :], -jnp.inf))
        a = jnp.where(strict, prod[:TILE] * decay, 0.0)
        qk = prod[TILE:] * decay
        inv = eye - a
        pw = a
        for _ in range(5):
            pw = _mm(pw, pw)
            inv = inv + _mm(inv, pw)
        uw = _mm(inv, jnp.concatenate([v * beta, kb * eg], axis=1))
        qd = qn * eg
        kd_t = (kn * ekd_all[:, h:h + 1]).T
        outs = []
        for ch in range(n_chunks):
            rows = slice(ch * c, (ch + 1) * c)
            pair_cols = slice((ch // 2) * LANES, (ch // 2 + 1) * LANES)
            first = ch % 2 == 0
            s_prev = state[h]
            ws_qs = _mm(jnp.concatenate([uw[rows, LANES:], qd[rows]], axis=0), s_prev)
            v_new = uw[rows, :LANES] - ws_qs[:c]
            v_pad = jnp.concatenate([v_new, zeros_c] if first else [zeros_c, v_new], axis=0)
            outs.append(ws_qs[c:] + _mm(qk[rows, pair_cols], v_pad))
            kd_m = jnp.where(lane_lo if first else ~lane_lo, kd_t[:, pair_cols], 0.0)
            dec = jnp.broadcast_to(egl_all[rows, h:h + 1], (c, LANES))
            state[h] = s_prev * jnp.concatenate([dec, dec], axis=0) + _mm(kd_m, v_pad)
        o = jnp.concatenate(outs, axis=0)
        o = o * lax.rsqrt(jnp.mean(o * o, axis=-1, keepdims=True) + NORM_EPS) * gng_ref[...]
        br_ref[2, :, hs] = (o * _silu(cz_ref[:, hs])).astype(BF16)
    cq_tail[...] = cqkv_ref[TILE - CONV_TAIL:, :]


def _mixer_kernel(final, x_ref, mg_ref, cqkv_ref, pv_ref, pz_ref, aq_ref, az_ref, cz_ref, ak_ref, av_ref,
                  cab_ref, poolw_ref, pscale_ref, sinks_ref, convw_ref, alog_ref, dtb_ref, gng_ref,
                  wbr_ref, wout_ref, fing_ref, o_ref,
                  pv_tail, cq_tail, kprev, vprev, state, br_ref, merged_ref):
    s_idx = pl.program_id(1)

    @pl.when(s_idx == 0)
    def _():
        pv_tail[...] = jnp.zeros_like(pv_tail)
        cq_tail[...] = jnp.zeros_like(cq_tail)
        kprev[...] = jnp.zeros_like(kprev)
        vprev[...] = jnp.zeros_like(vprev)
        state[...] = jnp.zeros_like(state)

    _pool_branch(s_idx, pv_ref, pz_ref, poolw_ref, pscale_ref, pv_tail, br_ref)
    _attn_branch(s_idx, aq_ref, ak_ref, av_ref, az_ref, sinks_ref, kprev, vprev, br_ref)
    _gdn_branch(cqkv_ref, cab_ref, cz_ref, convw_ref, alog_ref, dtb_ref, gng_ref, cq_tail, state, br_ref)

    half = D_MODEL // 2
    for c0 in range(0, D_MODEL, half):
        acc = None
        for n in range(3):
            up = jnp.dot(br_ref[n], wbr_ref[n, :, c0:c0 + half], preferred_element_type=F32)
            term = _sigmoid(mg_ref[:, n * D_MODEL + c0:n * D_MODEL + c0 + half]) * up
            acc = term if acc is None else acc + term
        merged_ref[:, c0:c0 + half] = acc.astype(BF16)
    y = x_ref[...] + jnp.dot(merged_ref[...], wout_ref[...], preferred_element_type=F32)
    if final:
        y = y * lax.rsqrt(jnp.mean(y * y, axis=-1, keepdims=True) + NORM_EPS) * fing_ref[...]
    o_ref[...] = y


def _mixer(xt, secs, poolw, pscale, sinks, convw, alog_row, dtb_row, gng_row, wbr, wout, fing_row,
           batch, final):
    t = xt.shape[0]
    tiles = t // batch // TILE
    row = lambda b, s: (b * tiles + s, 0)
    const2 = lambda b, s: (0, 0)
    const3 = lambda b, s: (0, 0, 0)
    sec_specs = [pl.BlockSpec((TILE, w), row) for _, w in _SECTIONS]
    in_specs = ([pl.BlockSpec((TILE, D_MODEL), row)] + sec_specs + [
        pl.BlockSpec(poolw.shape, const3),
        pl.BlockSpec(pscale.shape, const2),
        pl.BlockSpec(memory_space=pltpu.SMEM),
        pl.BlockSpec(convw.shape, const2),
        pl.BlockSpec(alog_row.shape, const2),
        pl.BlockSpec(dtb_row.shape, const2),
        pl.BlockSpec(gng_row.shape, const2),
        pl.BlockSpec(wbr.shape, const3),
        pl.BlockSpec(wout.shape, const2),
        pl.BlockSpec(fing_row.shape, const2),
    ])
    return pl.pallas_call(
        functools.partial(_mixer_kernel, final),
        grid=(batch, tiles),
        in_specs=in_specs,
        out_specs=pl.BlockSpec((TILE, D_MODEL), row),
        out_shape=jax.ShapeDtypeStruct((t, D_MODEL), F32),
        scratch_shapes=[
            pltpu.VMEM((POOL_TAIL, BRANCH_WIDTH), F32),
            pltpu.VMEM((CONV_TAIL, 3 * BRANCH_WIDTH), F32),
            pltpu.VMEM((ATTN_WINDOW, LANES), F32),
            pltpu.VMEM((ATTN_WINDOW, LANES), F32),
            pltpu.VMEM((GDN_HEADS, GDN_HEAD_DIM, GDN_HEAD_DIM), F32),
            pltpu.VMEM((3, TILE, BRANCH_WIDTH), BF16),
            pltpu.VMEM((TILE, D_MODEL), BF16),
        ],
        compiler_params=pltpu.CompilerParams(dimension_semantics=("arbitrary", "arbitrary"),
                                             vmem_limit_bytes=VMEM_LIMIT),
        name="mixer_final" if final else "mixer",
    )(xt, *secs, poolw, pscale, sinks, convw, alog_row, dtb_row, gng_row, wbr, wout, fing_row)


def _pack_w_in(w):
    parts = []
    for name, width in _SECTIONS:
        lo, hi = _COLS[name]
        part = w[:, lo:hi]
        if hi - lo < width:
            part = jnp.pad(part, ((0, 0), (0, width - (hi - lo))))
        parts.append(part)
    return jnp.concatenate(parts, axis=1).astype(BF16)


def _lane_row(v):
    return jnp.pad(v.astype(F32), (0, LANES - v.shape[0])).reshape(1, LANES)


def kernel(x, norm_g, w_in, pool_w, pool_scale, attn_sinks, conv_w, a_log, dt_bias, gdn_norm_g, w_branch,
           w_out, final_norm_g):
    batch, seq, d = x.shape
    assert d == D_MODEL and seq % TILE == 0 and (batch * seq) % TILE_IN == 0
    xt = x.reshape(batch * seq, d)
    for l in range(DEPTH):
        secs = _inproj(xt, norm_g[l].reshape(1, d), _pack_w_in(w_in[l]))
        xt = _mixer(xt, secs,
                    pool_w[l].astype(BF16), pool_scale[l].reshape(1, BRANCH_WIDTH), attn_sinks[l],
                    conv_w[l], _lane_row(a_log[l]), _lane_row(dt_bias[l]),
                    gdn_norm_g[l].reshape(1, GDN_HEAD_DIM),
                    w_branch[l].astype(BF16), w_out[l].astype(BF16), final_norm_g.reshape(1, d),
                    batch, final=(l == DEPTH - 1))
    return xt.reshape(batch, seq, d)
```

```python
import functools

import jax
import jax.numpy as jnp
from jax import lax
from jax.experimental import pallas as pl
from jax.experimental.pallas import tpu as pltpu

F32 = jnp.float32
BF16 = jnp.bfloat16

D_MODEL = 1024
DEPTH = 2
BRANCH_WIDTH = 512
POOL_WINDOWS = (2, 4, 8, 16)
POOL_CH = 128
ATTN_HEAD_DIM = 64
ATTN_HEADS = 8
ATTN_WINDOW = 128
GDN_HEAD_DIM = 128
GDN_HEADS = 4
GDN_CONV = 4
GDN_CHUNK = 64
NORM_EPS = 1e-6

_COLS = dict(pv=(0, 512), pz=(512, 1024), aq=(1024, 1536), ak=(1536, 1664), av=(1664, 1792),
             az=(1792, 2304), cqkv=(2304, 3840), cab=(3840, 3848), cz=(3848, 4360), mg=(4360, 7432))
_SECTIONS = (("mg", 3072), ("cqkv", 1536), ("pv", 512), ("pz", 512), ("aq", 512), ("az", 512),
             ("cz", 512), ("ak", 128), ("av", 128), ("cab", 128))
_PACKED_WIDTH = sum(w for _, w in _SECTIONS)

LANES = 128
TILE = 256
POOL_TAIL = 16
CONV_TAIL = 8
VMEM_LIMIT = 56 * 1024 * 1024


def _sigmoid(v):
    return 1.0 / (1.0 + jnp.exp(-v))


def _silu(v):
    return v * _sigmoid(v)


def _mm(a, b):
    return jnp.dot(a.astype(BF16), b.astype(BF16), preferred_element_type=F32)


def _mm_tb(a, b):
    return lax.dot_general(a.astype(BF16), b.astype(BF16), (((1,), (1,)), ((), ())),
                           preferred_element_type=F32)


def _in_proj(x_ref, g_ref, w_ref, sec_refs):
    x = x_ref[...]
    ms = jnp.mean(x * x, axis=-1, keepdims=True)
    h = (x * lax.rsqrt(ms + NORM_EPS) * g_ref[...]).astype(BF16)
    off = 0
    for ref, (_, width) in zip(sec_refs, _SECTIONS):
        for c in range(0, width, 512):
            cw = min(512, width - c)
            ref[:, c:c + cw] = jnp.dot(h, w_ref[:, off + c:off + c + cw],
                                       preferred_element_type=F32)
        off += width


def _pool_branch(s_idx, pv_ref, pz_ref, poolw_ref, pscale_ref, pv_tail, br_ref):
    pos = (s_idx * TILE + 1 + lax.broadcasted_iota(jnp.int32, (TILE, LANES), 0)).astype(F32)
    for gi, win in enumerate(POOL_WINDOWS):
        sl = slice(gi * POOL_CH, (gi + 1) * POOL_CH)
        cur = pv_ref[:, sl]
        acc = jnp.concatenate([pv_tail[:, sl], cur], axis=0)
        k = 1
        while k < win:
            acc = acc + pltpu.roll(acc, k, 0)
            k *= 2
        pooled = acc[POOL_TAIL:] / jnp.minimum(pos, float(win)) - cur
        mixed = _mm(pooled, poolw_ref[gi]) * pscale_ref[:, sl]
        br_ref[0, :, sl] = (mixed * _silu(pz_ref[:, sl])).astype(BF16)
        pv_tail[:, sl] = cur[TILE - POOL_TAIL:]


def _attn_branch(s_idx, aq_ref, ak_ref, av_ref, az_ref, sinks_ref, kprev, vprev, br_ref):
    w = ATTN_WINDOW
    k_all = jnp.concatenate([kprev[...], ak_ref[...]], axis=0)
    v_all = jnp.concatenate([vprev[...], av_ref[...]], axis=0)
    kprev[...] = ak_ref[TILE - w:, :]
    vprev[...] = av_ref[TILE - w:, :]
    k_sw = pltpu.roll(k_all, ATTN_HEAD_DIM, 1).astype(BF16)
    v_sw = pltpu.roll(v_all, ATTN_HEAD_DIM, 1).astype(BF16)
    k_all = k_all.astype(BF16)
    v_all = v_all.astype(BF16)

    qi = lax.broadcasted_iota(jnp.int32, (w, 2 * w), 0)
    kj = lax.broadcasted_iota(jnp.int32, (w, 2 * w), 1)
    dist = qi + w - kj
    in_window = (dist >= 0) & (dist < w)
    distf = dist.astype(F32)
    lane_lo = lax.broadcasted_iota(jnp.int32, (w, LANES), 1) < ATTN_HEAD_DIM
    scale = ATTN_HEAD_DIM ** -0.5

    for blk in range(TILE // w):
        rows = slice(blk * w, (blk + 1) * w)
        keys = slice(blk * w, blk * w + 2 * w)
        valid = in_window
        if blk == 0:
            valid = valid & ((kj >= w) | (s_idx > 0))
        for pair in range(ATTN_HEADS // 2):
            cols = slice(pair * LANES, (pair + 1) * LANES)
            q_pair = aq_ref[rows, cols]
            kv_head = pair // 2
            res = []
            for half in range(2):
                head = 2 * pair + half
                q_m = jnp.where(lane_lo if half == 0 else ~lane_lo, q_pair, 0.0)
                same = kv_head == half
                k_op = (k_all if same else k_sw)[keys]
                v_op = (v_all if same else v_sw)[keys]
                slope = 2.0 ** (-(head + 1))
                sc = _mm_tb(q_m, k_op) * scale - slope * distf
                sc = jnp.where(valid, sc, -jnp.inf)
                sink = sinks_ref[head]
                m = jnp.maximum(jnp.max(sc, axis=-1, keepdims=True), sink)
                p = jnp.exp(sc - m)
                denom = jnp.sum(p, axis=-1, keepdims=True) + jnp.exp(sink - m)
                res.append(_mm(p, v_op) / denom)
            o_pair = jnp.where(lane_lo, res[0], res[1])
            br_ref[1, rows, cols] = (o_pair * _silu(az_ref[rows, cols])).astype(BF16)


def _conv_silu(cqkv_ref, cq_tail, convw_ref, slab):
    sl = slice(slab * LANES, (slab + 1) * LANES)
    cur = cqkv_ref[:, sl]
    ext = jnp.concatenate([cq_tail[:, sl], cur], axis=0)
    acc = cur * convw_ref[GDN_CONV - 1:GDN_CONV, sl]
    for back in range(1, GDN_CONV):
        tap = GDN_CONV - 1 - back
        acc = acc + pltpu.roll(ext, back, 0)[CONV_TAIL:] * convw_ref[tap:tap + 1, sl]
    return _silu(acc)


def _l2_norm(v):
    return v * lax.rsqrt(jnp.sum(v * v, axis=-1, keepdims=True) + NORM_EPS)


def _gdn_branch(cqkv_ref, cab_ref, cz_ref, convw_ref, alog_ref, dtb_ref, gng_ref, cq_tail, state, br_ref):
    c = GDN_CHUNK
    n_chunks = TILE // c
    cab = cab_ref[...]
    sp_in = cab + dtb_ref[...]
    softplus = jnp.maximum(sp_in, 0.0) + jnp.log(1.0 + jnp.exp(-jnp.abs(sp_in)))
    g_all = -jnp.exp(alog_ref[...]) * softplus
    beta_all = _sigmoid(cab)
    row_in_chunk = lax.broadcasted_iota(jnp.int32, (TILE, LANES), 0) & (c - 1)
    gc_all = g_all
    k = 1
    while k < c:
        gc_all = gc_all + jnp.where(row_in_chunk >= k, pltpu.roll(gc_all, k, 0), 0.0)
        k *= 2
    gc_t = gc_all.T
    gl_all = jnp.broadcast_to(gc_all.reshape(n_chunks, c, LANES)[:, c - 1:c, :],
                              (n_chunks, c, LANES)).reshape(TILE, LANES)
    eg_all = jnp.exp(gc_all)
    ekd_all = jnp.exp(gl_all - gc_all)
    egl_all = jnp.exp(gl_all)

    ri = lax.broadcasted_iota(jnp.int32, (TILE, TILE), 0)
    ci = lax.broadcasted_iota(jnp.int32, (TILE, TILE), 1)
    same_chunk = (ri & -c) == (ci & -c)
    causal = same_chunk & (ri >= ci)
    strict = same_chunk & (ri > ci)
    eye = (ri == ci).astype(F32)
    lane_lo = lax.broadcasted_iota(jnp.int32, (LANES, LANES), 1) < c
    zeros_c = jnp.zeros((c, LANES), F32)
    scale = GDN_HEAD_DIM ** -0.5

    for h in range(GDN_HEADS):
        hs = slice(h * LANES, (h + 1) * LANES)
        qn = _l2_norm(_conv_silu(cqkv_ref, cq_tail, convw_ref, h)) * scale
        kn = _l2_norm(_conv_silu(cqkv_ref, cq_tail, convw_ref, GDN_HEADS + h))
        v = _conv_silu(cqkv_ref, cq_tail, convw_ref, 2 * GDN_HEADS + h)
        beta = beta_all[:, GDN_HEADS + h:GDN_HEADS + h + 1]
        gcol = gc_all[:, h:h + 1]
        eg = eg_all[:, h:h + 1]
        kb = kn * beta
        prod = _mm_tb(jnp.concatenate([kb, qn], axis=0), kn)
        decay = jnp.exp(jnp.where(causal, gcol - gc_t[h:h + 1, :], -jnp.inf))
        a = jnp.where(strict, prod[:TILE] * decay, 0.0)
        qk = prod[TILE:] * decay
        inv = eye - a
        pw = a
        for _ in range(5):
            pw = _mm(pw, pw)
            inv = inv + _mm(inv, pw)
        uw = _mm(inv, jnp.concatenate([v * beta, kb * eg], axis=1))
        qd = qn * eg
        kd_t = (kn * ekd_all[:, h:h + 1]).T
        outs = []
        for ch in range(n_chunks):
            rows = slice(ch * c, (ch + 1) * c)
            pair_cols = slice((ch // 2) * LANES, (ch // 2 + 1) * LANES)
            first = ch % 2 == 0
            s_prev = state[h]
            ws_qs = _mm(jnp.concatenate([uw[rows, LANES:], qd[rows]], axis=0), s_prev)
            v_new = uw[rows, :LANES] - ws_qs[:c]
            v_pad = jnp.concatenate([v_new, zeros_c] if first else [zeros_c, v_new], axis=0)
            outs.append(ws_qs[c:] + _mm(qk[rows, pair_cols], v_pad))
            kd_m = jnp.where(lane_lo if first else ~lane_lo, kd_t[:, pair_cols], 0.0)
            dec = jnp.broadcast_to(egl_all[rows, h:h + 1], (c, LANES))
            state[h] = s_prev * jnp.concatenate([dec, dec], axis=0) + _mm(kd_m, v_pad)
        o = jnp.concatenate(outs, axis=0)
        o = o * lax.rsqrt(jnp.mean(o * o, axis=-1, keepdims=True) + NORM_EPS) * gng_ref[...]
        br_ref[2, :, hs] = (o * _silu(cz_ref[:, hs])).astype(BF16)
    cq_tail[...] = cqkv_ref[TILE - CONV_TAIL:, :]


def _layer_kernel(final, x_ref, ng_ref, win_ref, poolw_ref, pscale_ref, sinks_ref, convw_ref, alog_ref,
                  dtb_ref, gng_ref, wbr_ref, wout_ref, fing_ref, o_ref,
                  mg_ref, cqkv_ref, pv_ref, pz_ref, aq_ref, az_ref, cz_ref, ak_ref, av_ref, cab_ref,
                  pv_tail, cq_tail, kprev, vprev, state, br_ref, merged_ref):
    s_idx = pl.program_id(1)

    @pl.when(s_idx == 0)
    def _():
        pv_tail[...] = jnp.zeros_like(pv_tail)
        cq_tail[...] = jnp.zeros_like(cq_tail)
        kprev[...] = jnp.zeros_like(kprev)
        vprev[...] = jnp.zeros_like(vprev)
        state[...] = jnp.zeros_like(state)

    _in_proj(x_ref, ng_ref, win_ref,
             (mg_ref, cqkv_ref, pv_ref, pz_ref, aq_ref, az_ref, cz_ref, ak_ref, av_ref, cab_ref))

    _pool_branch(s_idx, pv_ref, pz_ref, poolw_ref, pscale_ref, pv_tail, br_ref)
    _attn_branch(s_idx, aq_ref, ak_ref, av_ref, az_ref, sinks_ref, kprev, vprev, br_ref)
    _gdn_branch(cqkv_ref, cab_ref, cz_ref, convw_ref, alog_ref, dtb_ref, gng_ref, cq_tail, state, br_ref)

    half = D_MODEL // 2
    for c0 in range(0, D_MODEL, half):
        acc = None
        for n in range(3):
            up = jnp.dot(br_ref[n], wbr_ref[n, :, c0:c0 + half], preferred_element_type=F32)
            term = _sigmoid(mg_ref[:, n * D_MODEL + c0:n * D_MODEL + c0 + half]) * up
            acc = term if acc is None else acc + term
        merged_ref[:, c0:c0 + half] = acc.astype(BF16)
    y = x_ref[...] + jnp.dot(merged_ref[...], wout_ref[...], preferred_element_type=F32)
    if final:
        y = y * lax.rsqrt(jnp.mean(y * y, axis=-1, keepdims=True) + NORM_EPS) * fing_ref[...]
    o_ref[...] = y


def _layer(xt, ng_row, w_packed, poolw, pscale, sinks, convw, alog_row, dtb_row, gng_row, wbr, wout,
           fing_row, batch, final):
    t = xt.shape[0]
    tiles = t // batch // TILE
    row = lambda b, s: (b * tiles + s, 0)
    const2 = lambda b, s: (0, 0)
    const3 = lambda b, s: (0, 0, 0)
    once = pl.Buffered(1)
    in_specs = [
        pl.BlockSpec((TILE, D_MODEL), row),
        pl.BlockSpec(ng_row.shape, const2),
        pl.BlockSpec(w_packed.shape, const2, pipeline_mode=once),
        pl.BlockSpec(poolw.shape, const3),
        pl.BlockSpec(pscale.shape, const2),
        pl.BlockSpec(memory_space=pltpu.SMEM),
        pl.BlockSpec(convw.shape, const2),
        pl.BlockSpec(alog_row.shape, const2),
        pl.BlockSpec(dtb_row.shape, const2),
        pl.BlockSpec(gng_row.shape, const2),
        pl.BlockSpec(wbr.shape, const3, pipeline_mode=once),
        pl.BlockSpec(wout.shape, const2, pipeline_mode=once),
        pl.BlockSpec(fing_row.shape, const2),
    ]
    return pl.pallas_call(
        functools.partial(_layer_kernel, final),
        grid=(batch, tiles),
        in_specs=in_specs,
        out_specs=pl.BlockSpec((TILE, D_MODEL), row),
        out_shape=jax.ShapeDtypeStruct((t, D_MODEL), F32),
        scratch_shapes=[pltpu.VMEM((TILE, w), F32) for _, w in _SECTIONS] + [
            pltpu.VMEM((POOL_TAIL, BRANCH_WIDTH), F32),
            pltpu.VMEM((CONV_TAIL, 3 * BRANCH_WIDTH), F32),
            pltpu.VMEM((ATTN_WINDOW, LANES), F32),
            pltpu.VMEM((ATTN_WINDOW, LANES), F32),
            pltpu.VMEM((GDN_HEADS, GDN_HEAD_DIM, GDN_HEAD_DIM), F32),
            pltpu.VMEM((3, TILE, BRANCH_WIDTH), BF16),
            pltpu.VMEM((TILE, D_MODEL), BF16),
        ],
        compiler_params=pltpu.CompilerParams(dimension_semantics=("arbitrary", "arbitrary"),
                                             vmem_limit_bytes=VMEM_LIMIT),
        name="layer_final" if final else "layer",
    )(xt, ng_row, w_packed, poolw, pscale, sinks, convw, alog_row, dtb_row, gng_row, wbr, wout, fing_row)


def _pack_w_in(w):
    parts = []
    for name, width in _SECTIONS:
        lo, hi = _COLS[name]
        part = w[:, lo:hi]
        if hi - lo < width:
            part = jnp.pad(part, ((0, 0), (0, width - (hi - lo))))
        parts.append(part)
    return jnp.concatenate(parts, axis=1).astype(BF16)


def _lane_row(v):
    return jnp.pad(v.astype(F32), (0, LANES - v.shape[0])).reshape(1, LANES)


def kernel(x, norm_g, w_in, pool_w, pool_scale, attn_sinks, conv_w, a_log, dt_bias, gdn_norm_g, w_branch,
           w_out, final_norm_g):
    batch, seq, d = x.shape
    assert d == D_MODEL and seq % TILE == 0
    xt = x.reshape(batch * seq, d)
    for l in range(DEPTH):
        xt = _layer(xt, norm_g[l].reshape(1, d), _pack_w_in(w_in[l]),
                    pool_w[l].astype(BF16), pool_scale[l].reshape(1, BRANCH_WIDTH), attn_sinks[l],
                    conv_w[l], _lane_row(a_log[l]), _lane_row(dt_bias[l]),
                    gdn_norm_g[l].reshape(1, GDN_HEAD_DIM),
                    w_branch[l].astype(BF16), w_out[l].astype(BF16), final_norm_g.reshape(1, d),
                    batch, final=(l == DEPTH - 1))
    return xt.reshape(batch, seq, d)
```

```python
import functools

import jax
import jax.numpy as jnp
from jax import lax
from jax.experimental import pallas as pl
from jax.experimental.pallas import tpu as pltpu

F32 = jnp.float32
BF16 = jnp.bfloat16

D_MODEL = 1024
DEPTH = 2
BRANCH_WIDTH = 512
POOL_WINDOWS = (2, 4, 8, 16)
POOL_CH = 128
ATTN_HEAD_DIM = 64
ATTN_HEADS = 8
ATTN_GROUP = 4
ATTN_WINDOW = 128
GDN_HEAD_DIM = 128
GDN_HEADS = 4
GDN_CONV = 4
GDN_BLOCK = 128
NORM_EPS = 1e-6

_COLS = dict(pv=(0, 512), pz=(512, 1024), aq=(1024, 1536), ak=(1536, 1664), av=(1664, 1792),
             az=(1792, 2304), cqkv=(2304, 3840), cab=(3840, 3848), cz=(3848, 4360), mg=(4360, 7432))
_SECTIONS = (("mg", 3072), ("cqkv", 1536), ("pv", 512), ("pz", 512), ("aq", 512), ("az", 512),
             ("cz", 512), ("ak", 128), ("av", 128), ("cab", 128))
_PACKED_WIDTH = sum(w for _, w in _SECTIONS)

LANES = 128
TILE = 256
POOL_TAIL = 16
CONV_TAIL = 8
VMEM_LIMIT = 56 * 1024 * 1024


def _sigmoid(v):
    return 1.0 / (1.0 + jnp.exp(-v))


def _silu(v):
    return v * _sigmoid(v)


def _mm(a, b):
    return jnp.dot(a.astype(BF16), b.astype(BF16), preferred_element_type=F32)


def _mm_tb(a, b):
    return lax.dot_general(a.astype(BF16), b.astype(BF16), (((1,), (1,)), ((), ())),
                           preferred_element_type=F32)


def _in_proj(x_ref, g_ref, w_ref, sec_refs):
    x = x_ref[...]
    ms = jnp.mean(x * x, axis=-1, keepdims=True)
    h = (x * lax.rsqrt(ms + NORM_EPS) * g_ref[...]).astype(BF16)
    off = 0
    for ref, (_, width) in zip(sec_refs, _SECTIONS):
        for c in range(0, width, 512):
            cw = min(512, width - c)
            ref[:, c:c + cw] = jnp.dot(h, w_ref[:, off + c:off + c + cw],
                                       preferred_element_type=F32)
        off += width


def _pool_branch(s_idx, pv_ref, pz_ref, poolw_ref, pscale_ref, pv_tail, br_ref):
    pos = (s_idx * TILE + 1 + lax.broadcasted_iota(jnp.int32, (TILE, LANES), 0)).astype(F32)
    for gi, win in enumerate(POOL_WINDOWS):
        sl = slice(gi * POOL_CH, (gi + 1) * POOL_CH)
        cur = pv_ref[:, sl]
        acc = jnp.concatenate([pv_tail[:, sl], cur], axis=0)
        k = 1
        while k < win:
            acc = acc + pltpu.roll(acc, k, 0)
            k *= 2
        pooled = acc[POOL_TAIL:] / jnp.minimum(pos, float(win)) - cur
        mixed = _mm(pooled, poolw_ref[gi]) * pscale_ref[:, sl]
        br_ref[0, :, sl] = (mixed * _silu(pz_ref[:, sl])).astype(BF16)
        pv_tail[:, sl] = cur[TILE - POOL_TAIL:]


def _attn_branch(s_idx, aq_ref, ak_ref, av_ref, az_ref, sinks_ref, kprev, vprev, br_ref):
    w = ATTN_WINDOW
    k_all = jnp.concatenate([kprev[...], ak_ref[...]], axis=0)
    v_all = jnp.concatenate([vprev[...], av_ref[...]], axis=0)
    kprev[...] = ak_ref[TILE - w:, :]
    vprev[...] = av_ref[TILE - w:, :]
    k_sw = pltpu.roll(k_all, ATTN_HEAD_DIM, 1).astype(BF16)
    v_sw = pltpu.roll(v_all, ATTN_HEAD_DIM, 1).astype(BF16)
    k_all = k_all.astype(BF16)
    v_all = v_all.astype(BF16)

    qi = lax.broadcasted_iota(jnp.int32, (w, 2 * w), 0)
    kj = lax.broadcasted_iota(jnp.int32, (w, 2 * w), 1)
    dist = qi + w - kj
    in_window = (dist >= 0) & (dist < w)
    distf = dist.astype(F32)
    lane_lo = lax.broadcasted_iota(jnp.int32, (w, LANES), 1) < ATTN_HEAD_DIM
    scale = ATTN_HEAD_DIM ** -0.5

    for blk in range(TILE // w):
        rows = slice(blk * w, (blk + 1) * w)
        keys = slice(blk * w, blk * w + 2 * w)
        valid = in_window
        if blk == 0:
            valid = valid & ((kj >= w) | (s_idx > 0))
        res = [None] * ATTN_HEADS
        for same in (True, False):
            heads = [h for h in range(ATTN_HEADS) if ((h // ATTN_GROUP) == (h % 2)) == same]
            k_op = (k_all if same else k_sw)[keys]
            v_op = (v_all if same else v_sw)[keys]
            q_stack = jnp.concatenate(
                [jnp.where(lane_lo if h % 2 == 0 else ~lane_lo,
                           aq_ref[rows, (h // 2) * LANES:(h // 2 + 1) * LANES] * scale, 0.0).astype(BF16)
                 for h in heads], axis=0)
            sc_all = _mm_tb(q_stack, k_op)
            probs, denoms = [], []
            for i, h in enumerate(heads):
                slope = 2.0 ** (-(h + 1))
                sc = jnp.where(valid, sc_all[i * w:(i + 1) * w] - slope * distf, -jnp.inf)
                sink = sinks_ref[h]
                m = jnp.maximum(jnp.max(sc, axis=-1, keepdims=True), sink)
                p = jnp.exp(sc - m)
                denoms.append(jnp.sum(p, axis=-1, keepdims=True) + jnp.exp(sink - m))
                probs.append(p.astype(BF16))
            o_all = _mm(jnp.concatenate(probs, axis=0), v_op)
            for i, h in enumerate(heads):
                res[h] = o_all[i * w:(i + 1) * w] / denoms[i]
        for pair in range(ATTN_HEADS // 2):
            cols = slice(pair * LANES, (pair + 1) * LANES)
            o_pair = jnp.where(lane_lo, res[2 * pair], res[2 * pair + 1])
            br_ref[1, rows, cols] = (o_pair * _silu(az_ref[rows, cols])).astype(BF16)


def _conv_silu(cqkv_ref, cq_tail, convw_ref, slab):
    sl = slice(slab * LANES, (slab + 1) * LANES)
    cur = cqkv_ref[:, sl]
    ext = jnp.concatenate([cq_tail[:, sl], cur], axis=0)
    acc = cur * convw_ref[GDN_CONV - 1:GDN_CONV, sl]
    for back in range(1, GDN_CONV):
        tap = GDN_CONV - 1 - back
        acc = acc + pltpu.roll(ext, back, 0)[CONV_TAIL:] * convw_ref[tap:tap + 1, sl]
    return _silu(acc)


def _l2_norm(v):
    return v * lax.rsqrt(jnp.sum(v * v, axis=-1, keepdims=True) + NORM_EPS)


def _gdn_branch(cqkv_ref, cab_ref, cz_ref, convw_ref, alog_ref, dtb_ref, gng_ref, cq_tail, state, br_ref):
    c = GDN_BLOCK
    n_chunks = TILE // c
    heads = range(GDN_HEADS)
    cab = cab_ref[...]
    sp_in = cab + dtb_ref[...]
    softplus = jnp.maximum(sp_in, 0.0) + jnp.log(1.0 + jnp.exp(-jnp.abs(sp_in)))
    g_all = -jnp.exp(alog_ref[...]) * softplus
    beta_all = _sigmoid(cab)
    row_in_chunk = lax.broadcasted_iota(jnp.int32, (TILE, LANES), 0) & (c - 1)
    gc_all = g_all
    k = 1
    while k < c:
        gc_all = gc_all + jnp.where(row_in_chunk >= k, pltpu.roll(gc_all, k, 0), 0.0)
        k *= 2
    gc_t = gc_all.T
    gl_all = jnp.broadcast_to(gc_all.reshape(n_chunks, c, LANES)[:, c - 1:c, :],
                              (n_chunks, c, LANES)).reshape(TILE, LANES)
    eg_all = jnp.exp(gc_all)
    ekd_all = jnp.exp(gl_all - gc_all)
    egl_all = jnp.exp(gl_all)

    ri = lax.broadcasted_iota(jnp.int32, (TILE, TILE), 0)
    ci = lax.broadcasted_iota(jnp.int32, (TILE, TILE), 1)
    same_chunk = (ri & -c) == (ci & -c)
    causal = same_chunk & (ri >= ci)
    strict = same_chunk & (ri > ci)
    eye = (ri == ci).astype(F32)
    scale = GDN_HEAD_DIM ** -0.5

    a, qk, rhs, qd, kd_t = [], [], [], [], []
    for h in heads:
        qn = _l2_norm(_conv_silu(cqkv_ref, cq_tail, convw_ref, h)) * scale
        kn = _l2_norm(_conv_silu(cqkv_ref, cq_tail, convw_ref, GDN_HEADS + h))
        v = _conv_silu(cqkv_ref, cq_tail, convw_ref, 2 * GDN_HEADS + h)
        beta = beta_all[:, GDN_HEADS + h:GDN_HEADS + h + 1]
        eg = eg_all[:, h:h + 1]
        kb = kn * beta
        prod = _mm_tb(jnp.concatenate([kb, qn], axis=0), kn)
        decay = jnp.exp(jnp.where(causal, gc_all[:, h:h + 1] - gc_t[h:h + 1, :], -jnp.inf))
        a.append(jnp.where(strict, prod[:TILE] * decay, 0.0))
        qk.append((prod[TILE:] * decay).astype(BF16))
        rhs.append(jnp.concatenate([v * beta, kb * eg], axis=1).astype(BF16))
        qd.append(qn * eg)
        kd_t.append((kn * ekd_all[:, h:h + 1]).T.astype(BF16))

    rc = ri ^ ci
    inv = [eye - jnp.where(rc == 1, a_h, 0.0) for a_h in a]
    s = 2
    while s < c:
        link = (rc & -s) == s
        tl = [_mm(i, jnp.where(link, a_h, 0.0)) for i, a_h in zip(inv, a)]
        inv = [i - _mm(t, i) for i, t in zip(inv, tl)]
        s *= 2
    uw = [_mm(i, r) for i, r in zip(inv, rhs)]

    s_cur = [state[h] for h in heads]
    outs = [[] for _ in heads]
    for ch in range(n_chunks):
        rows = slice(ch * c, (ch + 1) * c)
        for h in heads:
            ws_qs = _mm(jnp.concatenate([uw[h][rows, LANES:], qd[h][rows]], axis=0), s_cur[h])
            v_new = (uw[h][rows, :LANES] - ws_qs[:c]).astype(BF16)
            outs[h].append(ws_qs[c:] + _mm(qk[h][rows, rows], v_new))
            dec = jnp.broadcast_to(egl_all[rows, h:h + 1], (c, LANES))
            s_cur[h] = s_cur[h] * dec + _mm(kd_t[h][:, rows], v_new)
    for h in heads:
        hs = slice(h * LANES, (h + 1) * LANES)
        state[h] = s_cur[h]
        o = jnp.concatenate(outs[h], axis=0)
        o = o * lax.rsqrt(jnp.mean(o * o, axis=-1, keepdims=True) + NORM_EPS) * gng_ref[...]
        br_ref[2, :, hs] = (o * _silu(cz_ref[:, hs])).astype(BF16)
    cq_tail[...] = cqkv_ref[TILE - CONV_TAIL:, :]


def _layer_kernel(final, x_ref, ng_ref, win_ref, poolw_ref, pscale_ref, sinks_ref, convw_ref, alog_ref,
                  dtb_ref, gng_ref, wbr_ref, wout_ref, fing_ref, o_ref,
                  mg_ref, cqkv_ref, pv_ref, pz_ref, aq_ref, az_ref, cz_ref, ak_ref, av_ref, cab_ref,
                  pv_tail, cq_tail, kprev, vprev, state, br_ref, merged_ref):
    s_idx = pl.program_id(1)

    @pl.when(s_idx == 0)
    def _():
        pv_tail[...] = jnp.zeros_like(pv_tail)
        cq_tail[...] = jnp.zeros_like(cq_tail)
        kprev[...] = jnp.zeros_like(kprev)
        vprev[...] = jnp.zeros_like(vprev)
        state[...] = jnp.zeros_like(state)

    _in_proj(x_ref, ng_ref, win_ref,
             (mg_ref, cqkv_ref, pv_ref, pz_ref, aq_ref, az_ref, cz_ref, ak_ref, av_ref, cab_ref))
    _pool_branch(s_idx, pv_ref, pz_ref, poolw_ref, pscale_ref, pv_tail, br_ref)
    _attn_branch(s_idx, aq_ref, ak_ref, av_ref, az_ref, sinks_ref, kprev, vprev, br_ref)
    _gdn_branch(cqkv_ref, cab_ref, cz_ref, convw_ref, alog_ref, dtb_ref, gng_ref, cq_tail, state, br_ref)

    half = D_MODEL // 2
    for c0 in range(0, D_MODEL, half):
        acc = None
        for n in range(3):
            up = jnp.dot(br_ref[n], wbr_ref[n, :, c0:c0 + half], preferred_element_type=F32)
            term = _sigmoid(mg_ref[:, n * D_MODEL + c0:n * D_MODEL + c0 + half]) * up
            acc = term if acc is None else acc + term
        merged_ref[:, c0:c0 + half] = acc.astype(BF16)
    y = x_ref[...] + jnp.dot(merged_ref[...], wout_ref[...], preferred_element_type=F32)
    if final:
        y = y * lax.rsqrt(jnp.mean(y * y, axis=-1, keepdims=True) + NORM_EPS) * fing_ref[...]
    o_ref[...] = y


def _layer(xt, ng_row, w_packed, poolw, pscale, sinks, convw, alog_row, dtb_row, gng_row, wbr, wout,
           fing_row, batch, final):
    t = xt.shape[0]
    tiles = t // batch // TILE
    row = lambda b, s: (b * tiles + s, 0)
    const2 = lambda b, s: (0, 0)
    const3 = lambda b, s: (0, 0, 0)
    once = pl.Buffered(1)
    in_specs = [
        pl.BlockSpec((TILE, D_MODEL), row),
        pl.BlockSpec(ng_row.shape, const2),
        pl.BlockSpec(w_packed.shape, const2, pipeline_mode=once),
        pl.BlockSpec(poolw.shape, const3),
        pl.BlockSpec(pscale.shape, const2),
        pl.BlockSpec(memory_space=pltpu.SMEM),
        pl.BlockSpec(convw.shape, const2),
        pl.BlockSpec(alog_row.shape, const2),
        pl.BlockSpec(dtb_row.shape, const2),
        pl.BlockSpec(gng_row.shape, const2),
        pl.BlockSpec(wbr.shape, const3, pipeline_mode=once),
        pl.BlockSpec(wout.shape, const2, pipeline_mode=once),
        pl.BlockSpec(fing_row.shape, const2),
    ]
    return pl.pallas_call(
        functools.partial(_layer_kernel, final),
        grid=(batch, tiles),
        in_specs=in_specs,
        out_specs=pl.BlockSpec((TILE, D_MODEL), row),
        out_shape=jax.ShapeDtypeStruct((t, D_MODEL), F32),
        scratch_shapes=[pltpu.VMEM((TILE, w), F32) for _, w in _SECTIONS] + [
            pltpu.VMEM((POOL_TAIL, BRANCH_WIDTH), F32),
            pltpu.VMEM((CONV_TAIL, 3 * BRANCH_WIDTH), F32),
            pltpu.VMEM((ATTN_WINDOW, LANES), F32),
            pltpu.VMEM((ATTN_WINDOW, LANES), F32),
            pltpu.VMEM((GDN_HEADS, GDN_HEAD_DIM, GDN_HEAD_DIM), F32),
            pltpu.VMEM((3, TILE, BRANCH_WIDTH), BF16),
            pltpu.VMEM((TILE, D_MODEL), BF16),
        ],
        compiler_params=pltpu.CompilerParams(dimension_semantics=("arbitrary", "arbitrary"),
                                             vmem_limit_bytes=VMEM_LIMIT),
        name="layer_final" if final else "layer",
    )(xt, ng_row, w_packed, poolw, pscale, sinks, convw, alog_row, dtb_row, gng_row, wbr, wout, fing_row)


def _pack_w_in(w):
    parts = []
    for name, width in _SECTIONS:
        lo, hi = _COLS[name]
        part = w[:, lo:hi]
        if hi - lo < width:
            part = jnp.pad(part, ((0, 0), (0, width - (hi - lo))))
        parts.append(part)
    return jnp.concatenate(parts, axis=1).astype(BF16)


def _lane_row(v):
    return jnp.pad(v.astype(F32), (0, LANES - v.shape[0])).reshape(1, LANES)


def kernel(x, norm_g, w_in, pool_w, pool_scale, attn_sinks, conv_w, a_log, dt_bias, gdn_norm_g, w_branch,
           w_out, final_norm_g):
    batch, seq, d = x.shape
    assert d == D_MODEL and seq % TILE == 0
    xt = x.reshape(batch * seq, d)
    for l in range(DEPTH):
        xt = _layer(xt, norm_g[l].reshape(1, d), _pack_w_in(w_in[l]),
                    pool_w[l].astype(BF16), pool_scale[l].reshape(1, BRANCH_WIDTH), attn_sinks[l],
                    conv_w[l], _lane_row(a_log[l]), _lane_row(dt_bias[l]),
                    gdn_norm_g[l].reshape(1, GDN_HEAD_DIM),
                    w_branch[l].astype(BF16), w_out[l].astype(BF16), final_norm_g.reshape(1, d),
                    batch, final=(l == DEPTH - 1))
    return xt.reshape(batch, seq, d)
```

```python
import functools

import jax
import jax.numpy as jnp
from jax import lax
from jax.experimental import pallas as pl
from jax.experimental.pallas import tpu as pltpu

F32 = jnp.float32
BF16 = jnp.bfloat16

D_MODEL = 1024
DEPTH = 2
BRANCH_WIDTH = 512
POOL_WINDOWS = (2, 4, 8, 16)
POOL_CH = 128
ATTN_HEAD_DIM = 64
ATTN_HEADS = 8
ATTN_GROUP = 4
ATTN_WINDOW = 128
GDN_HEAD_DIM = 128
GDN_HEADS = 4
GDN_CONV = 4
GDN_BLOCK = 128
NORM_EPS = 1e-6
LOG2E = 1.4426950408889634

_COLS = dict(pv=(0, 512), pz=(512, 1024), aq=(1024, 1536), akv=(1536, 1792),
             az=(1792, 2304), cqkv=(2304, 3840), cab=(3840, 3848), cz=(3848, 4360), mg=(4360, 7432))
_SECTIONS = (("mg", 3072), ("cqkv", 1536), ("pv", 512), ("pz", 512), ("aq", 512), ("az", 512),
             ("cz", 512), ("akv", 256), ("cab", 128))
_PACKED_WIDTH = sum(w for _, w in _SECTIONS)

LANES = 128
TILE = 256
POOL_TAIL = 16
CONV_TAIL = 8
VMEM_LIMIT = 60 * 1024 * 1024


def _sigmoid(v):
    return 0.5 * jnp.tanh(0.5 * v) + 0.5


def _silu(v):
    hv = 0.5 * v
    return hv * (jnp.tanh(hv) + 1.0)


def _mm(a, b):
    return jnp.dot(a.astype(BF16), b.astype(BF16), preferred_element_type=F32)


def _mm_tb(a, b):
    return lax.dot_general(a.astype(BF16), b.astype(BF16), (((1,), (1,)), ((), ())),
                           preferred_element_type=F32)


def _run(*gens):
    for g in gens:
        for _ in g:
            pass


VECTOR_BOUND, MATMUL_BOUND = True, False


def _weave(main, filler):
    for vector_bound in main:
        if vector_bound:
            next(filler, None)
    _run(filler)


def _in_proj_steps(x_ref, rows, g_ref, w_ref, h_ref, dst):
    x = x_ref[rows, :]
    ms = jnp.mean(x * x, axis=-1, keepdims=True)
    h_ref[...] = (x * lax.rsqrt(ms + NORM_EPS) * g_ref[...]).astype(BF16)
    yield
    off = 0
    for name, width in _SECTIONS:
        for c in range(0, width, 512):
            cw = min(512, width - c)
            dst[name][:, c:c + cw] = jnp.dot(h_ref[...], w_ref[:, off + c:off + c + cw],
                                             preferred_element_type=F32)
            yield
        off += width


def _pool_steps(s_idx, pv_ref, pz_ref, poolw_ref, pscale_ref, pv_tail, br_ref):
    pos = (s_idx * TILE + 1 + lax.broadcasted_iota(jnp.int32, (TILE, LANES), 0)).astype(F32)
    pooled = []
    for gi, win in enumerate(POOL_WINDOWS):
        sl = slice(gi * POOL_CH, (gi + 1) * POOL_CH)
        cur = pv_ref[:, sl]
        acc = jnp.concatenate([pv_tail[:, sl], cur], axis=0)
        k = 1
        while k < win:
            acc = acc + pltpu.roll(acc, k, 0)
            k *= 2
        pooled.append((acc[POOL_TAIL:] / jnp.minimum(pos, float(win)) - cur).astype(BF16))
        pv_tail[:, sl] = cur[TILE - POOL_TAIL:]
        if gi % 2 == 1:
            sl2 = slice((gi - 1) * POOL_CH, (gi + 1) * POOL_CH)
            mixed = _mm(jnp.concatenate(pooled[-2:], axis=1), poolw_ref[gi // 2]) * pscale_ref[:, sl2]
            br_ref[0, :, sl2] = (mixed * _silu(pz_ref[:, sl2])).astype(BF16)
            yield VECTOR_BOUND


def _attn_steps(first_in_seq, aq_ref, akv_ref, az_ref, sinks_ref, kprev, vprev, br_ref):
    w = ATTN_WINDOW
    k_all = jnp.concatenate([kprev[...], akv_ref[:, :LANES]], axis=0)
    v_all = jnp.concatenate([vprev[...], akv_ref[:, LANES:]], axis=0)
    kprev[...] = akv_ref[TILE - w:, :LANES]
    vprev[...] = akv_ref[TILE - w:, LANES:]
    k_sw = pltpu.roll(k_all, ATTN_HEAD_DIM, 1).astype(BF16)
    v_sw = pltpu.roll(v_all, ATTN_HEAD_DIM, 1).astype(BF16)
    k_all = k_all.astype(BF16)
    v_all = v_all.astype(BF16)

    qi = lax.broadcasted_iota(jnp.int32, (w, 2 * w), 0)
    kj = lax.broadcasted_iota(jnp.int32, (w, 2 * w), 1)
    dist = qi + w - kj
    in_window = (dist >= 0) & (dist < w)
    distf = dist.astype(F32)
    lane_lo = lax.broadcasted_iota(jnp.int32, (w, LANES), 1) < ATTN_HEAD_DIM
    scale = ATTN_HEAD_DIM ** -0.5 * LOG2E
    bias = {}

    for blk in range(TILE // w):
        rows = slice(blk * w, (blk + 1) * w)
        keys = slice(blk * w, blk * w + 2 * w)
        valid = in_window
        if blk == 0 and first_in_seq is not None:
            valid = valid & ((kj >= w) | jnp.logical_not(first_in_seq))
        res = [None] * ATTN_HEADS
        for same in (True, False):
            heads = [h for h in range(ATTN_HEADS) if ((h // ATTN_GROUP) == (h % 2)) == same]
            k_op = (k_all if same else k_sw)[keys]
            v_op = (v_all if same else v_sw)[keys]
            q_stack = jnp.concatenate(
                [jnp.where(lane_lo if h % 2 == 0 else ~lane_lo,
                           aq_ref[rows, (h // 2) * LANES:(h // 2 + 1) * LANES] * scale, 0.0).astype(BF16)
                 for h in heads], axis=0)
            sc_all = _mm_tb(q_stack, k_op)
            probs, denoms = [], []
            for i, h in enumerate(heads):
                if h not in bias:
                    bias[h] = (2.0 ** (-(h + 1)) * LOG2E) * distf
                sc = jnp.where(valid, sc_all[i * w:(i + 1) * w] - bias[h], -jnp.inf)
                sink = sinks_ref[h] * LOG2E
                m = jnp.maximum(jnp.max(sc, axis=-1, keepdims=True), sink)
                p = jnp.exp2(sc - m)
                denoms.append(jnp.sum(p, axis=-1, keepdims=True) + jnp.exp2(sink - m))
                probs.append(p.astype(BF16))
            o_all = _mm(jnp.concatenate(probs, axis=0), v_op)
            for i, h in enumerate(heads):
                res[h] = o_all[i * w:(i + 1) * w] / denoms[i]
            yield VECTOR_BOUND
        for pair in range(ATTN_HEADS // 2):
            cols = slice(pair * LANES, (pair + 1) * LANES)
            o_pair = jnp.where(lane_lo, res[2 * pair], res[2 * pair + 1])
            br_ref[1, rows, cols] = (o_pair * _silu(az_ref[rows, cols])).astype(BF16)
        yield VECTOR_BOUND


def _conv_silu(cq_ext, convw_ref, slab):
    sl = slice(slab * LANES, (slab + 1) * LANES)
    acc = None
    for back in range(GDN_CONV):
        tap = GDN_CONV - 1 - back
        term = cq_ext[CONV_TAIL - back:CONV_TAIL - back + TILE, sl] * convw_ref[tap:tap + 1, sl]
        acc = term if acc is None else acc + term
    return _silu(acc)


def _l2_norm(v):
    return v * lax.rsqrt(jnp.sum(v * v, axis=-1, keepdims=True) + NORM_EPS)


def _gdn_steps(cq_ext, cq_next, cab_ref, cz_ref, convw_ref, alog_ref, dtb_ref, gng_ref, state, br_ref):
    c = GDN_BLOCK
    n_chunks = TILE // c
    heads = range(GDN_HEADS)
    x8 = cab_ref[...].T[0:8]
    sp_in = x8 + dtb_ref[...]
    softplus = jnp.maximum(sp_in, 0.0) + jnp.log(1.0 + jnp.exp(-jnp.abs(sp_in)))
    gc_t = (-LOG2E) * jnp.exp(alog_ref[...]) * softplus
    lane_in_chunk = lax.broadcasted_iota(jnp.int32, (8, TILE), 1) & (c - 1)
    k = 1
    while k < c:
        gc_t = gc_t + jnp.where(lane_in_chunk >= k, pltpu.roll(gc_t, k, 1), 0.0)
        k *= 2
    gl_t = jnp.concatenate([jnp.broadcast_to(gc_t[:, ch * c + c - 1:(ch + 1) * c], (8, c))
                            for ch in range(n_chunks)], axis=1)
    stack = jnp.concatenate([gc_t, _sigmoid(x8), jnp.exp2(gc_t), jnp.exp2(gl_t - gc_t), jnp.exp2(gl_t),
                             jnp.zeros((LANES - 40, TILE), F32)], axis=0)
    cols = stack.T
    gc_all, beta_all, eg_all, ekd_all, egl_all = (cols[:, 8 * i:8 * i + 8] for i in range(5))

    ri = lax.broadcasted_iota(jnp.int32, (TILE, TILE), 0)
    ci = lax.broadcasted_iota(jnp.int32, (TILE, TILE), 1)
    same_chunk = (ri & -c) == (ci & -c)
    causal = same_chunk & (ri >= ci)
    strict = same_chunk & (ri > ci)
    eye = (ri == ci).astype(F32)
    scale = GDN_HEAD_DIM ** -0.5
    yield VECTOR_BOUND

    a, qk, rhs, qd, kd_t = [], [], [], [], []
    for h in heads:
        qn = _l2_norm(_conv_silu(cq_ext, convw_ref, h)) * scale
        kn = _l2_norm(_conv_silu(cq_ext, convw_ref, GDN_HEADS + h))
        v = _conv_silu(cq_ext, convw_ref, 2 * GDN_HEADS + h)
        beta = beta_all[:, GDN_HEADS + h:GDN_HEADS + h + 1]
        eg = eg_all[:, h:h + 1]
        kb = kn * beta
        prod = _mm_tb(jnp.concatenate([kb, qn], axis=0), kn)
        decay = jnp.exp2(jnp.where(causal, gc_all[:, h:h + 1] - gc_t[h:h + 1, :], -jnp.inf))
        a.append(jnp.where(strict, prod[:TILE] * decay, 0.0))
        qk.append((prod[TILE:] * decay).astype(BF16))
        rhs.append(jnp.concatenate([v * beta, kb * eg], axis=1).astype(BF16))
        qd.append(qn * eg)
        kd_t.append((kn * ekd_all[:, h:h + 1]).T.astype(BF16))
        yield VECTOR_BOUND

    rc = ri ^ ci
    inv = [eye - jnp.where(rc == 1, a_h, 0.0) for a_h in a]
    s = 2
    while s < c:
        link = (rc & -s) == s
        lnk = [jnp.where(link, a_h, 0.0).astype(BF16) for a_h in a]
        if s < 8:
            tl = [_mm(i, l) for i, l in zip(inv, lnk)]
            inv = [i - _mm(t, i) for i, t in zip(inv, tl)]
        else:
            lower = [slice(b + s, b + 2 * s) for b in range(0, TILE, 2 * s)]
            upper = [slice(b, b + s) for b in range(0, TILE, 2 * s)]
            tl = [_mm(jnp.concatenate([i[r] for r in lower], axis=0), l) for i, l in zip(inv, lnk)]
            upd = [_mm(t, i) for i, t in zip(inv, tl)]
            inv = [jnp.concatenate([piece for n, (u, lo) in enumerate(zip(upper, lower))
                                    for piece in (i[u], i[lo] - d[n * s:(n + 1) * s])], axis=0)
                   for i, d in zip(inv, upd)]
        s *= 2
        yield MATMUL_BOUND
    uw = [_mm(i, r) for i, r in zip(inv, rhs)]
    yield MATMUL_BOUND

    s_cur = [state[h] for h in heads]
    outs = [[] for _ in heads]
    for ch in range(n_chunks):
        rows = slice(ch * c, (ch + 1) * c)
        for h in heads:
            ws_qs = _mm(jnp.concatenate([uw[h][rows, LANES:], qd[h][rows]], axis=0), s_cur[h])
            v_new = (uw[h][rows, :LANES] - ws_qs[:c]).astype(BF16)
            outs[h].append(ws_qs[c:] + _mm(qk[h][rows, rows], v_new))
            dec = jnp.broadcast_to(egl_all[rows, h:h + 1], (c, LANES))
            s_cur[h] = s_cur[h] * dec + _mm(kd_t[h][:, rows], v_new)
        yield MATMUL_BOUND
    for h in heads:
        hs = slice(h * LANES, (h + 1) * LANES)
        state[h] = s_cur[h]
        o = jnp.concatenate(outs[h], axis=0)
        o = o * lax.rsqrt(jnp.mean(o * o, axis=-1, keepdims=True) + NORM_EPS) * gng_ref[...]
        br_ref[2, :, hs] = (o * _silu(cz_ref[:, hs])).astype(BF16)
        yield VECTOR_BOUND
    cq_next[0:CONV_TAIL, :] = cq_ext[TILE:, :]


def _merge_steps(final, x_ref, rows, mg_ref, wbr_ref, wout_ref, fing_ref, br_ref, merged_ref, o_ref):
    half = D_MODEL // 2
    for c0 in range(0, D_MODEL, half):
        acc = None
        for n in range(3):
            up = jnp.dot(br_ref[n], wbr_ref[n, :, c0:c0 + half], preferred_element_type=F32)
            term = _sigmoid(mg_ref[:, n * D_MODEL + c0:n * D_MODEL + c0 + half]) * up
            acc = term if acc is None else acc + term
        merged_ref[:, c0:c0 + half] = acc.astype(BF16)
        yield MATMUL_BOUND
    y = x_ref[rows, :] + jnp.dot(merged_ref[...], wout_ref[...], preferred_element_type=F32)
    if final:
        y = y * lax.rsqrt(jnp.mean(y * y, axis=-1, keepdims=True) + NORM_EPS) * fing_ref[...]
    o_ref[rows, :] = y
    yield MATMUL_BOUND


def _layer_kernel(final, tiles_per_seq, x_ref, xn_ref, ng_ref, win_ref, poolw_ref, pscale_ref, sinks_ref,
                  convw_ref, alog_ref, dtb_ref, gng_ref, wbr_ref, wout_ref, fing_ref, o_ref, *scratch):
    n_sec = len(_SECTIONS)
    names = [name for name, _ in _SECTIONS]
    sec = [dict(zip(names, scratch[b * n_sec:(b + 1) * n_sec])) for b in range(2)]
    (pv_tail, kprev, vprev, state, h0, h1, br0, br1, merged0, merged1) = scratch[2 * n_sec:]
    hbuf, br, merged = (h0, h1), (br0, br1), (merged0, merged1)
    step = pl.program_id(0)
    s_idx = (2 * step) % tiles_per_seq
    tile_rows = (slice(0, TILE), slice(TILE, 2 * TILE))

    @pl.when(s_idx == 0)
    def _():
        pv_tail[...] = jnp.zeros_like(pv_tail)
        sec[0]["cqkv"][0:CONV_TAIL, :] = jnp.zeros((CONV_TAIL, 3 * BRANCH_WIDTH), F32)
        kprev[...] = jnp.zeros_like(kprev)
        vprev[...] = jnp.zeros_like(vprev)
        state[...] = jnp.zeros_like(state)

    def project(src_ref, rows, buf):
        dst = dict(sec[buf], cqkv=sec[buf]["cqkv"].at[pl.ds(CONV_TAIL, TILE)])
        return _in_proj_steps(src_ref, rows, ng_ref, win_ref, hbuf[buf], dst)

    @pl.when(step == 0)
    def _():
        _run(project(x_ref, tile_rows[0], 0))

    def mixers(buf):
        s = sec[buf]
        first_in_seq = (s_idx == 0) if buf == 0 else None
        yield from _pool_steps(s_idx + buf, s["pv"], s["pz"], poolw_ref, pscale_ref, pv_tail, br[buf])
        yield from _attn_steps(first_in_seq, s["aq"], s["akv"], s["az"], sinks_ref, kprev, vprev, br[buf])
        yield from _gdn_steps(s["cqkv"], sec[1 - buf]["cqkv"], s["cab"], s["cz"], convw_ref, alog_ref, dtb_ref,
                              gng_ref, state, br[buf])
        yield from _merge_steps(final, x_ref, tile_rows[buf], s["mg"], wbr_ref, wout_ref, fing_ref, br[buf],
                                merged[buf], o_ref)

    _weave(mixers(0), project(x_ref, tile_rows[1], 1))
    _weave(mixers(1), project(xn_ref, slice(None), 0))


def _layer(xt, ng_row, w_packed, poolw, pscale, sinks, convw, alog_row, dtb_row, gng_row, wbr, wout,
           fing_row, batch, final):
    t = xt.shape[0]
    n_tiles = t // TILE
    tiles_per_seq = n_tiles // batch
    assert tiles_per_seq % 2 == 0
    const2 = lambda i: (0, 0)
    const3 = lambda i: (0, 0, 0)
    once = pl.Buffered(1)
    in_specs = [
        pl.BlockSpec((2 * TILE, D_MODEL), lambda i: (i, 0)),
        pl.BlockSpec((TILE, D_MODEL), lambda i: (jnp.minimum(2 * i + 2, n_tiles - 1), 0)),
        pl.BlockSpec(ng_row.shape, const2),
        pl.BlockSpec(w_packed.shape, const2, pipeline_mode=once),
        pl.BlockSpec(poolw.shape, const3),
        pl.BlockSpec(pscale.shape, const2),
        pl.BlockSpec(memory_space=pltpu.SMEM),
        pl.BlockSpec(convw.shape, const2),
        pl.BlockSpec(alog_row.shape, const2),
        pl.BlockSpec(dtb_row.shape, const2),
        pl.BlockSpec(gng_row.shape, const2),
        pl.BlockSpec(wbr.shape, const3, pipeline_mode=once),
        pl.BlockSpec(wout.shape, const2, pipeline_mode=once),
        pl.BlockSpec(fing_row.shape, const2),
    ]
    sec_scratch = [pltpu.VMEM((TILE + (CONV_TAIL if name == "cqkv" else 0), w), F32) for name, w in _SECTIONS]
    return pl.pallas_call(
        functools.partial(_layer_kernel, final, tiles_per_seq),
        grid=(n_tiles // 2,),
        in_specs=in_specs,
        out_specs=pl.BlockSpec((2 * TILE, D_MODEL), lambda i: (i, 0)),
        out_shape=jax.ShapeDtypeStruct((t, D_MODEL), F32),
        scratch_shapes=sec_scratch + sec_scratch + [
            pltpu.VMEM((POOL_TAIL, BRANCH_WIDTH), F32),
            pltpu.VMEM((ATTN_WINDOW, LANES), F32),
            pltpu.VMEM((ATTN_WINDOW, LANES), F32),
            pltpu.VMEM((GDN_HEADS, GDN_HEAD_DIM, GDN_HEAD_DIM), F32),
            pltpu.VMEM((TILE, D_MODEL), BF16),
            pltpu.VMEM((TILE, D_MODEL), BF16),
            pltpu.VMEM((3, TILE, BRANCH_WIDTH), BF16),
            pltpu.VMEM((3, TILE, BRANCH_WIDTH), BF16),
            pltpu.VMEM((TILE, D_MODEL), BF16),
            pltpu.VMEM((TILE, D_MODEL), BF16),
        ],
        compiler_params=pltpu.CompilerParams(dimension_semantics=("arbitrary",),
                                             vmem_limit_bytes=VMEM_LIMIT),
        name="layer_final" if final else "layer",
    )(xt, xt, ng_row, w_packed, poolw, pscale, sinks, convw, alog_row, dtb_row, gng_row, wbr, wout, fing_row)


def _pack_w_in(w):
    parts = []
    for name, width in _SECTIONS:
        lo, hi = _COLS[name]
        part = w[:, lo:hi]
        if hi - lo < width:
            part = jnp.pad(part, ((0, 0), (0, width - (hi - lo))))
        parts.append(part)
    return jnp.concatenate(parts, axis=1).astype(BF16)


def _head_col(v):
    return jnp.pad(v.astype(F32), (0, 8 - v.shape[0])).reshape(8, 1)


def _pool_pairs(w):
    z = jnp.zeros_like(w[0])
    return jnp.stack([jnp.block([[w[2 * p], z], [z, w[2 * p + 1]]]) for p in range(len(POOL_WINDOWS) // 2)]
                     ).astype(BF16)


def kernel(x, norm_g, w_in, pool_w, pool_scale, attn_sinks, conv_w, a_log, dt_bias, gdn_norm_g, w_branch,
           w_out, final_norm_g):
    batch, seq, d = x.shape
    assert d == D_MODEL and seq % (2 * TILE) == 0
    xt = x.reshape(batch * seq, d)
    for l in range(DEPTH):
        xt = _layer(xt, norm_g[l].reshape(1, d), _pack_w_in(w_in[l]),
                    _pool_pairs(pool_w[l]), pool_scale[l].reshape(1, BRANCH_WIDTH), attn_sinks[l],
                    conv_w[l], _head_col(a_log[l]), _head_col(dt_bias[l]),
                    gdn_norm_g[l].reshape(1, GDN_HEAD_DIM),
                    w_branch[l].astype(BF16), w_out[l].astype(BF16), final_norm_g.reshape(1, d),
                    batch, final=(l == DEPTH - 1))
    return xt.reshape(batch, seq, d)
```

```python
import functools

import jax
import jax.numpy as jnp
from jax import lax
from jax.experimental import pallas as pl
from jax.experimental.pallas import tpu as pltpu

F32 = jnp.float32
BF16 = jnp.bfloat16

D_MODEL = 1024
DEPTH = 2
BRANCH_WIDTH = 512
POOL_WINDOWS = (2, 4, 8, 16)
POOL_CH = 128
ATTN_HEAD_DIM = 64
ATTN_HEADS = 8
ATTN_GROUP = 4
ATTN_WINDOW = 128
GDN_HEAD_DIM = 128
GDN_HEADS = 4
GDN_CONV = 4
GDN_BLOCK = 128
NORM_EPS = 1e-6
LOG2E = 1.4426950408889634

_COLS = dict(pv=(0, 512), pz=(512, 1024), aq=(1024, 1536), akv=(1536, 1792),
             az=(1792, 2304), cqkv=(2304, 3840), cab=(3840, 3848), cz=(3848, 4360), mg=(4360, 7432))
_SECTIONS = (("mg", 3072), ("cqkv", 1536), ("pv", 512), ("pz", 512), ("aq", 512), ("az", 512),
             ("cz", 512), ("akv", 256), ("cab", 128))
_PACKED_WIDTH = sum(w for _, w in _SECTIONS)
_STEP_SECTIONS = ("mg", "cqkv", "cz", "cab")
_TILE_SECTIONS = ("pv", "pz", "aq", "az", "akv")

LANES = 128
TILE = 256
STEP_TILES = 1
STEP_ROWS = STEP_TILES * TILE
PROJ_CHUNK = 512
POOL_TAIL = 16
CONV_TAIL = 8
VMEM_LIMIT = 58 * 1024 * 1024


def _sigmoid(v):
    return 0.5 * jnp.tanh(0.5 * v) + 0.5


def _silu(v):
    hv = 0.5 * v
    return hv * (jnp.tanh(hv) + 1.0)


def _mm(a, b):
    return jnp.dot(a.astype(BF16), b.astype(BF16), preferred_element_type=F32)


def _mm_tb(a, b):
    return lax.dot_general(a.astype(BF16), b.astype(BF16), (((1,), (1,)), ((), ())),
                           preferred_element_type=F32)


def _run(*gens):
    for g in gens:
        for _ in g:
            pass


def _normalise(x_ref, g_ref, h_ref):
    x = x_ref[...]
    ms = jnp.mean(x * x, axis=-1, keepdims=True)
    h_ref[...] = (x * lax.rsqrt(ms + NORM_EPS) * g_ref[...]).astype(BF16)


def _in_proj_steps(names, tile_ids, w_ref, h_ref, dst):
    off = 0
    for name, width in _SECTIONS:
        if name in names:
            for c in range(0, width, PROJ_CHUNK):
                cw = min(PROJ_CHUNK, width - c)
                for t in tile_ids:
                    dst[name, t][:, c:c + cw] = jnp.dot(h_ref[t * TILE:(t + 1) * TILE, :],
                                                        w_ref[:, off + c:off + c + cw],
                                                        preferred_element_type=F32)
                yield
        off += width


def _pool_steps(s_idx, pv_ref, pz_ref, poolw_ref, pscale_ref, pv_tail, br_ref):
    pos = (s_idx * TILE + 1 + lax.broadcasted_iota(jnp.int32, (TILE, LANES), 0)).astype(F32)
    pooled = []
    for gi, win in enumerate(POOL_WINDOWS):
        sl = slice(gi * POOL_CH, (gi + 1) * POOL_CH)
        cur = pv_ref[:, sl]
        acc = jnp.concatenate([pv_tail[:, sl], cur], axis=0)
        k = 1
        while k < win:
            acc = acc + pltpu.roll(acc, k, 0)
            k *= 2
        pooled.append((acc[POOL_TAIL:] / jnp.minimum(pos, float(win)) - cur).astype(BF16))
        pv_tail[:, sl] = cur[TILE - POOL_TAIL:]
        if gi % 2 == 1:
            sl2 = slice((gi - 1) * POOL_CH, (gi + 1) * POOL_CH)
            mixed = _mm(jnp.concatenate(pooled[-2:], axis=1), poolw_ref[gi // 2]) * pscale_ref[:, sl2]
            br_ref[0, :, sl2] = (mixed * _silu(pz_ref[:, sl2])).astype(BF16)
            yield


def _attn_steps(first_in_seq, aq_ref, akv_ref, az_ref, sinks_ref, kprev, vprev, br_ref):
    w = ATTN_WINDOW
    k_all = jnp.concatenate([kprev[...], akv_ref[:, :LANES]], axis=0)
    v_all = jnp.concatenate([vprev[...], akv_ref[:, LANES:]], axis=0)
    kprev[...] = akv_ref[TILE - w:, :LANES]
    vprev[...] = akv_ref[TILE - w:, LANES:]
    k_sw = pltpu.roll(k_all, ATTN_HEAD_DIM, 1).astype(BF16)
    v_sw = pltpu.roll(v_all, ATTN_HEAD_DIM, 1).astype(BF16)
    k_all = k_all.astype(BF16)
    v_all = v_all.astype(BF16)

    qi = lax.broadcasted_iota(jnp.int32, (w, 2 * w), 0)
    kj = lax.broadcasted_iota(jnp.int32, (w, 2 * w), 1)
    dist = qi + w - kj
    in_window = (dist >= 0) & (dist < w)
    distf = dist.astype(F32)
    lane_lo = lax.broadcasted_iota(jnp.int32, (w, LANES), 1) < ATTN_HEAD_DIM
    scale = ATTN_HEAD_DIM ** -0.5 * LOG2E
    bias = {}

    for blk in range(TILE // w):
        rows = slice(blk * w, (blk + 1) * w)
        keys = slice(blk * w, blk * w + 2 * w)
        valid = in_window
        if blk == 0 and first_in_seq is not None:
            valid = valid & ((kj >= w) | jnp.logical_not(first_in_seq))
        res = [None] * ATTN_HEADS
        for same in (True, False):
            heads = [h for h in range(ATTN_HEADS) if ((h // ATTN_GROUP) == (h % 2)) == same]
            k_op = (k_all if same else k_sw)[keys]
            v_op = (v_all if same else v_sw)[keys]
            q_stack = jnp.concatenate(
                [jnp.where(lane_lo if h % 2 == 0 else ~lane_lo,
                           aq_ref[rows, (h // 2) * LANES:(h // 2 + 1) * LANES] * scale, 0.0).astype(BF16)
                 for h in heads], axis=0)
            sc_all = _mm_tb(q_stack, k_op)
            probs, denoms = [], []
            for i, h in enumerate(heads):
                if h not in bias:
                    bias[h] = (2.0 ** (-(h + 1)) * LOG2E) * distf
                sc = jnp.where(valid, sc_all[i * w:(i + 1) * w] - bias[h], -jnp.inf)
                sink = sinks_ref[h] * LOG2E
                m = jnp.maximum(jnp.max(sc, axis=-1, keepdims=True), sink)
                p = jnp.exp2(sc - m)
                denoms.append(jnp.sum(p, axis=-1, keepdims=True) + jnp.exp2(sink - m))
                probs.append(p.astype(BF16))
            o_all = _mm(jnp.concatenate(probs, axis=0), v_op)
            for i, h in enumerate(heads):
                res[h] = o_all[i * w:(i + 1) * w] / denoms[i]
            yield
        for pair in range(ATTN_HEADS // 2):
            cols = slice(pair * LANES, (pair + 1) * LANES)
            o_pair = jnp.where(lane_lo, res[2 * pair], res[2 * pair + 1])
            br_ref[1, rows, cols] = (o_pair * _silu(az_ref[rows, cols])).astype(BF16)
        yield


def _conv_silu(cq_ext, convw_ref, slab):
    sl = slice(slab * LANES, (slab + 1) * LANES)
    acc = None
    for back in range(GDN_CONV):
        tap = GDN_CONV - 1 - back
        term = cq_ext[CONV_TAIL - back:CONV_TAIL - back + TILE, sl] * convw_ref[tap:tap + 1, sl]
        acc = term if acc is None else acc + term
    return _silu(acc)


def _l2_norm(v):
    return v * lax.rsqrt(jnp.sum(v * v, axis=-1, keepdims=True) + NORM_EPS)


def _gdn_steps(tiles, convw_ref, alog_ref, dtb_ref, gng_ref, state):
    c = GDN_BLOCK
    n_chunks = TILE // c
    heads = range(GDN_HEADS)
    ri = lax.broadcasted_iota(jnp.int32, (TILE, TILE), 0)
    ci = lax.broadcasted_iota(jnp.int32, (TILE, TILE), 1)
    same_chunk = (ri & -c) == (ci & -c)
    causal = same_chunk & (ri >= ci)
    strict = same_chunk & (ri > ci)
    eye = (ri == ci).astype(F32)
    scale = GDN_HEAD_DIM ** -0.5
    lane_in_chunk = lax.broadcasted_iota(jnp.int32, (8, TILE), 1) & (c - 1)

    a, qk, rhs, qd, kd_t, egl = [], [], [], [], [], []
    for tile in tiles:
        x8 = tile["cab"][...].T[0:8]
        sp_in = x8 + dtb_ref[...]
        softplus = jnp.maximum(sp_in, 0.0) + jnp.log(1.0 + jnp.exp(-jnp.abs(sp_in)))
        gc_t = (-LOG2E) * jnp.exp(alog_ref[...]) * softplus
        k = 1
        while k < c:
            gc_t = gc_t + jnp.where(lane_in_chunk >= k, pltpu.roll(gc_t, k, 1), 0.0)
            k *= 2
        gl_t = jnp.concatenate([jnp.broadcast_to(gc_t[:, ch * c + c - 1:(ch + 1) * c], (8, c))
                                for ch in range(n_chunks)], axis=1)
        stack = jnp.concatenate([gc_t, _sigmoid(x8), jnp.exp2(gc_t), jnp.exp2(gl_t - gc_t), jnp.exp2(gl_t),
                                 jnp.zeros((LANES - 40, TILE), F32)], axis=0)
        cols = stack.T
        gc_all, beta_all, eg_all, ekd_all, egl_all = (cols[:, 8 * i:8 * i + 8] for i in range(5))
        yield
        for h in heads:
            qn = _l2_norm(_conv_silu(tile["cq_ext"], convw_ref, h)) * scale
            kn = _l2_norm(_conv_silu(tile["cq_ext"], convw_ref, GDN_HEADS + h))
            v = _conv_silu(tile["cq_ext"], convw_ref, 2 * GDN_HEADS + h)
            beta = beta_all[:, GDN_HEADS + h:GDN_HEADS + h + 1]
            eg = eg_all[:, h:h + 1]
            kb = kn * beta
            prod = _mm_tb(jnp.concatenate([kb, qn], axis=0), kn)
            decay = jnp.exp2(jnp.where(causal, gc_all[:, h:h + 1] - gc_t[h:h + 1, :], -jnp.inf))
            a.append(jnp.where(strict, prod[:TILE] * decay, 0.0).astype(BF16))
            qk.append((prod[TILE:] * decay).astype(BF16))
            rhs.append(jnp.concatenate([v * beta, kb * eg], axis=1).astype(BF16))
            qd.append(qn * eg)
            kd_t.append((kn * ekd_all[:, h:h + 1]).T.astype(BF16))
            egl.append(egl_all[:, h:h + 1])
            yield

    rc = ri ^ ci
    zero = jnp.zeros((), BF16)
    inv = [eye - jnp.where(rc == 1, a_u, zero) for a_u in a]
    s = 2
    while s < c:
        link = (rc & -s) == s
        lnk = [jnp.where(link, a_u, zero) for a_u in a]
        if s < 8:
            tl = [_mm(i, l) for i, l in zip(inv, lnk)]
            inv = [i - _mm(t, i) for i, t in zip(inv, tl)]
        else:
            lower = [slice(b + s, b + 2 * s) for b in range(0, TILE, 2 * s)]
            upper = [slice(b, b + s) for b in range(0, TILE, 2 * s)]
            tl = [_mm(jnp.concatenate([i[r] for r in lower], axis=0), l) for i, l in zip(inv, lnk)]
            upd = [_mm(t, i) for i, t in zip(inv, tl)]
            inv = [jnp.concatenate([piece for n, (u, lo) in enumerate(zip(upper, lower))
                                    for piece in (i[u], i[lo] - d[n * s:(n + 1) * s])], axis=0)
                   for i, d in zip(inv, upd)]
        s *= 2
        yield
    uw = [_mm(i, r) for i, r in zip(inv, rhs)]
    yield

    eye_c = lax.broadcasted_iota(jnp.int32, (c, c), 0) == lax.broadcasted_iota(jnp.int32, (c, c), 1)
    lhs = [[None] * n_chunks for _ in a]
    add = [[None] * n_chunks for _ in a]
    for ch in range(n_chunks):
        rows = slice(ch * c, (ch + 1) * c)
        for u in range(len(a)):
            uw_c = uw[u][rows].astype(BF16)
            k_uw = _mm(kd_t[u][:, rows], uw_c)
            q_uw = _mm(qk[u][rows, rows], uw_c)
            dec = jnp.broadcast_to(egl[u][rows], (c, LANES))
            lhs[u][ch] = jnp.concatenate([jnp.where(eye_c, dec, 0.0) - k_uw[:, LANES:],
                                          qd[u][rows] - q_uw[:, LANES:]], axis=0).astype(BF16)
            add[u][ch] = (k_uw[:, :LANES], q_uw[:, :LANES])
        yield
    s_cur = [state[h] for h in heads]
    outs = [[] for _ in a]
    for t in range(len(tiles)):
        for ch in range(n_chunks):
            for h in heads:
                u = t * GDN_HEADS + h
                prod_s = _mm(lhs[u][ch], s_cur[h])
                outs[u].append(prod_s[c:] + add[u][ch][1])
                s_cur[h] = prod_s[:c] + add[u][ch][0]
            yield
    for h in heads:
        state[h] = s_cur[h]
    for t, tile in enumerate(tiles):
        for h in heads:
            hs = slice(h * LANES, (h + 1) * LANES)
            o = jnp.concatenate(outs[t * GDN_HEADS + h], axis=0)
            o = o * lax.rsqrt(jnp.mean(o * o, axis=-1, keepdims=True) + NORM_EPS) * gng_ref[...]
            tile["br"][2, :, hs] = (o * _silu(tile["cz"][:, hs])).astype(BF16)
            yield


def _merge_steps(final, x_ref, mg_ref, wbr_ref, wout_ref, fing_ref, br_ref, merged_ref, o_ref):
    half = D_MODEL // 2
    for c0 in range(0, D_MODEL, half):
        acc = None
        for n in range(3):
            up = jnp.dot(br_ref[n], wbr_ref[n, :, c0:c0 + half], preferred_element_type=F32)
            term = _sigmoid(mg_ref[:, n * D_MODEL + c0:n * D_MODEL + c0 + half]) * up
            acc = term if acc is None else acc + term
        merged_ref[:, c0:c0 + half] = acc.astype(BF16)
        yield
    y = x_ref[...] + jnp.dot(merged_ref[...], wout_ref[...], preferred_element_type=F32)
    if final:
        y = y * lax.rsqrt(jnp.mean(y * y, axis=-1, keepdims=True) + NORM_EPS) * fing_ref[...]
    o_ref[...] = y
    yield


def _layer_kernel(final, tiles_per_seq, x_ref, ng_ref, win_ref, poolw_ref, pscale_ref, sinks_ref,
                  convw_ref, alog_ref, dtb_ref, gng_ref, wbr_ref, wout_ref, fing_ref, o_ref, *scratch):
    n_sec = len(_SECTIONS)
    sec = dict(zip([name for name, _ in _SECTIONS], scratch[:n_sec]))
    (pv_tail, kprev, vprev, state, hbuf, br) = scratch[n_sec:]
    merged = hbuf
    cq = sec["cqkv"]
    s_idx = (STEP_TILES * pl.program_id(0)) % tiles_per_seq

    @pl.when(s_idx == 0)
    def _():
        pv_tail[...] = jnp.zeros_like(pv_tail)
        cq[0:CONV_TAIL, :] = jnp.zeros((CONV_TAIL, 3 * BRANCH_WIDTH), F32)
        kprev[...] = jnp.zeros_like(kprev)
        vprev[...] = jnp.zeros_like(vprev)
        state[...] = jnp.zeros_like(state)

    tile_ids = range(STEP_TILES)
    dst = {}
    for name in _STEP_SECTIONS:
        base = CONV_TAIL if name == "cqkv" else 0
        for t in tile_ids:
            dst[name, t] = sec[name].at[pl.ds(base + t * TILE, TILE)]
    for name in _TILE_SECTIONS:
        for t in tile_ids:
            dst[name, t] = sec[name]
    _normalise(x_ref, ng_ref, hbuf)
    _run(_in_proj_steps(_STEP_SECTIONS, tile_ids, win_ref, hbuf, dst))

    tiles = []
    for t in tile_ids:
        tile = dict(cz=dst["cz", t], cab=dst["cab", t], br=br.at[:, pl.ds(t * TILE, TILE), :],
                    cq_ext=cq.at[pl.ds(t * TILE, TILE + CONV_TAIL)])
        tiles.append(tile)
        first_in_seq = (s_idx == 0) if t == 0 else None
        _run(_in_proj_steps(_TILE_SECTIONS, (t,), win_ref, hbuf, dst),
             _pool_steps(s_idx + t, sec["pv"], sec["pz"], poolw_ref, pscale_ref, pv_tail, tile["br"]),
             _attn_steps(first_in_seq, sec["aq"], sec["akv"], sec["az"], sinks_ref, kprev, vprev, tile["br"]))
    _run(_gdn_steps(tiles, convw_ref, alog_ref, dtb_ref, gng_ref, state))
    cq[0:CONV_TAIL, :] = cq[STEP_ROWS:, :]
    _run(_merge_steps(final, x_ref, sec["mg"], wbr_ref, wout_ref, fing_ref, br, merged, o_ref))


def _layer(xt, ng_row, w_packed, poolw, pscale, sinks, convw, alog_row, dtb_row, gng_row, wbr, wout,
           fing_row, batch, final):
    t = xt.shape[0]
    n_tiles = t // TILE
    tiles_per_seq = n_tiles // batch
    assert tiles_per_seq % STEP_TILES == 0
    const2 = lambda i: (0, 0)
    const3 = lambda i: (0, 0, 0)
    once = pl.Buffered(1)
    in_specs = [
        pl.BlockSpec((STEP_ROWS, D_MODEL), lambda i: (i, 0)),
        pl.BlockSpec(ng_row.shape, const2),
        pl.BlockSpec(w_packed.shape, const2, pipeline_mode=once),
        pl.BlockSpec(poolw.shape, const3),
        pl.BlockSpec(pscale.shape, const2),
        pl.BlockSpec(memory_space=pltpu.SMEM),
        pl.BlockSpec(convw.shape, const2),
        pl.BlockSpec(alog_row.shape, const2),
        pl.BlockSpec(dtb_row.shape, const2),
        pl.BlockSpec(gng_row.shape, const2),
        pl.BlockSpec(wbr.shape, const3, pipeline_mode=once),
        pl.BlockSpec(wout.shape, const2, pipeline_mode=once),
        pl.BlockSpec(fing_row.shape, const2),
    ]
    sec_scratch = [pltpu.VMEM(((STEP_ROWS if name in _STEP_SECTIONS else TILE)
                               + (CONV_TAIL if name == "cqkv" else 0), w), F32) for name, w in _SECTIONS]
    return pl.pallas_call(
        functools.partial(_layer_kernel, final, tiles_per_seq),
        grid=(n_tiles // STEP_TILES,),
        in_specs=in_specs,
        out_specs=pl.BlockSpec((STEP_ROWS, D_MODEL), lambda i: (i, 0)),
        out_shape=jax.ShapeDtypeStruct((t, D_MODEL), F32),
        scratch_shapes=sec_scratch + [
            pltpu.VMEM((POOL_TAIL, BRANCH_WIDTH), F32),
            pltpu.VMEM((ATTN_WINDOW, LANES), F32),
            pltpu.VMEM((ATTN_WINDOW, LANES), F32),
            pltpu.VMEM((GDN_HEADS, GDN_HEAD_DIM, GDN_HEAD_DIM), F32),
            pltpu.VMEM((STEP_ROWS, D_MODEL), BF16),
            pltpu.VMEM((3, STEP_ROWS, BRANCH_WIDTH), BF16),
        ],
        compiler_params=pltpu.CompilerParams(dimension_semantics=("arbitrary",),
                                             vmem_limit_bytes=VMEM_LIMIT),
        name="layer_final" if final else "layer",
    )(xt, ng_row, w_packed, poolw, pscale, sinks, convw, alog_row, dtb_row, gng_row, wbr, wout, fing_row)


def _pack_w_in(w):
    parts = []
    for name, width in _SECTIONS:
        lo, hi = _COLS[name]
        part = w[:, lo:hi]
        if hi - lo < width:
            part = jnp.pad(part, ((0, 0), (0, width - (hi - lo))))
        parts.append(part)
    return jnp.concatenate(parts, axis=1).astype(BF16)


def _head_col(v):
    return jnp.pad(v.astype(F32), (0, 8 - v.shape[0])).reshape(8, 1)


def _pool_pairs(w):
    z = jnp.zeros_like(w[0])
    return jnp.stack([jnp.block([[w[2 * p], z], [z, w[2 * p + 1]]]) for p in range(len(POOL_WINDOWS) // 2)]
                     ).astype(BF16)


def kernel(x, norm_g, w_in, pool_w, pool_scale, attn_sinks, conv_w, a_log, dt_bias, gdn_norm_g, w_branch,
           w_out, final_norm_g):
    batch, seq, d = x.shape
    assert d == D_MODEL and seq % STEP_ROWS == 0
    xt = x.reshape(batch * seq, d)
    for l in range(DEPTH):
        xt = _layer(xt, norm_g[l].reshape(1, d), _pack_w_in(w_in[l]),
                    _pool_pairs(pool_w[l]), pool_scale[l].reshape(1, BRANCH_WIDTH), attn_sinks[l],
                    conv_w[l], _head_col(a_log[l]), _head_col(dt_bias[l]),
                    gdn_norm_g[l].reshape(1, GDN_HEAD_DIM),
                    w_branch[l].astype(BF16), w_out[l].astype(BF16), final_norm_g.reshape(1, d),
                    batch, final=(l == DEPTH - 1))
    return xt.reshape(batch, seq, d)
```

```python
import functools

import jax
import jax.numpy as jnp
from jax import lax
from jax.experimental import pallas as pl
from jax.experimental.pallas import tpu as pltpu

F32 = jnp.float32
BF16 = jnp.bfloat16

D_MODEL = 1024
DEPTH = 2
BRANCH_WIDTH = 512
POOL_WINDOWS = (2, 4, 8, 16)
POOL_CH = 128
ATTN_HEAD_DIM = 64
ATTN_HEADS = 8
ATTN_GROUP = 4
ATTN_WINDOW = 128
GDN_HEAD_DIM = 128
GDN_HEADS = 4
GDN_CONV = 4
GDN_BLOCK = 128
NORM_EPS = 1e-6
LOG2E = 1.4426950408889634

_COLS = dict(pv=(0, 512), pz=(512, 1024), aq=(1024, 1536), akv=(1536, 1792),
             az=(1792, 2304), cqkv=(2304, 3840), cab=(3840, 3848), cz=(3848, 4360), mg=(4360, 7432))
_SECTIONS = (("mg", 3072), ("cqkv", 1536), ("cz", 512), ("cab", 128), ("pv", 512), ("pz", 512),
             ("aq", 512), ("az", 512), ("akv", 256))

LANES = 128
TILE = 256
PROJ_CHUNK = 512
POOL_TAIL = 16
CONV_TAIL = 8
VMEM_LIMIT = 56 * 1024 * 1024


def _sigmoid(v):
    return 0.5 * jnp.tanh(0.5 * v) + 0.5


def _silu(v):
    hv = 0.5 * v
    return hv * (jnp.tanh(hv) + 1.0)


def _mm(a, b):
    return jnp.dot(a.astype(BF16), b.astype(BF16), preferred_element_type=F32)


def _mm_tb(a, b):
    return lax.dot_general(a.astype(BF16), b.astype(BF16), (((1,), (1,)), ((), ())),
                           preferred_element_type=F32)


def _in_proj(x_ref, g_ref, w_refs, h_ref, dst):
    x = x_ref[...]
    ms = jnp.mean(x * x, axis=-1, keepdims=True)
    h_ref[...] = (x * lax.rsqrt(ms + NORM_EPS) * g_ref[...]).astype(BF16)
    for name, width in _SECTIONS:
        for c in range(0, width, PROJ_CHUNK):
            cw = min(PROJ_CHUNK, width - c)
            dst[name][:, c:c + cw] = jnp.dot(h_ref[...], w_refs[name][:, c:c + cw],
                                             preferred_element_type=F32)


def _pool_branch(s_idx, pv_ref, pz_ref, poolw_ref, pscale_ref, pv_tail, br_ref):
    pos = (s_idx * TILE + 1 + lax.broadcasted_iota(jnp.int32, (TILE, LANES), 0)).astype(F32)
    pooled = []
    for gi, win in enumerate(POOL_WINDOWS):
        sl = slice(gi * POOL_CH, (gi + 1) * POOL_CH)
        cur = pv_ref[:, sl]
        acc = jnp.concatenate([pv_tail[:, sl], cur], axis=0)
        k = 1
        while k < win:
            acc = acc + pltpu.roll(acc, k, 0)
            k *= 2
        pooled.append((acc[POOL_TAIL:] / jnp.minimum(pos, float(win)) - cur).astype(BF16))
        pv_tail[:, sl] = cur[TILE - POOL_TAIL:]
        if gi % 2 == 1:
            sl2 = slice((gi - 1) * POOL_CH, (gi + 1) * POOL_CH)
            mixed = _mm(jnp.concatenate(pooled[-2:], axis=1), poolw_ref[gi // 2]) * pscale_ref[:, sl2]
            br_ref[0, :, sl2] = (mixed * _silu(pz_ref[:, sl2])).astype(BF16)


def _attn_branch(first_in_seq, aq_ref, akv_ref, az_ref, sinks_ref, kprev, vprev, br_ref):
    w = ATTN_WINDOW
    k_all = jnp.concatenate([kprev[...], akv_ref[:, :LANES]], axis=0)
    v_all = jnp.concatenate([vprev[...], akv_ref[:, LANES:]], axis=0)
    kprev[...] = akv_ref[TILE - w:, :LANES]
    vprev[...] = akv_ref[TILE - w:, LANES:]
    k_sw = pltpu.roll(k_all, ATTN_HEAD_DIM, 1).astype(BF16)
    v_sw = pltpu.roll(v_all, ATTN_HEAD_DIM, 1).astype(BF16)
    k_all = k_all.astype(BF16)
    v_all = v_all.astype(BF16)

    qi = lax.broadcasted_iota(jnp.int32, (w, 2 * w), 0)
    kj = lax.broadcasted_iota(jnp.int32, (w, 2 * w), 1)
    dist = qi + w - kj
    in_window = (dist >= 0) & (dist < w)
    distf = dist.astype(F32)
    lane_lo = lax.broadcasted_iota(jnp.int32, (w, LANES), 1) < ATTN_HEAD_DIM
    scale = ATTN_HEAD_DIM ** -0.5 * LOG2E
    bias = {}

    for blk in range(TILE // w):
        rows = slice(blk * w, (blk + 1) * w)
        keys = slice(blk * w, blk * w + 2 * w)
        valid = in_window
        if blk == 0:
            valid = valid & ((kj >= w) | jnp.logical_not(first_in_seq))
        res = [None] * ATTN_HEADS
        for same in (True, False):
            heads = [h for h in range(ATTN_HEADS) if ((h // ATTN_GROUP) == (h % 2)) == same]
            k_op = (k_all if same else k_sw)[keys]
            v_op = (v_all if same else v_sw)[keys]
            q_stack = jnp.concatenate(
                [jnp.where(lane_lo if h % 2 == 0 else ~lane_lo,
                           aq_ref[rows, (h // 2) * LANES:(h // 2 + 1) * LANES] * scale, 0.0).astype(BF16)
                 for h in heads], axis=0)
            sc_all = _mm_tb(q_stack, k_op)
            probs, denoms = [], []
            for i, h in enumerate(heads):
                if h not in bias:
                    bias[h] = (2.0 ** (-(h + 1)) * LOG2E) * distf
                sc = jnp.where(valid, sc_all[i * w:(i + 1) * w] - bias[h], -jnp.inf)
                sink = sinks_ref[h] * LOG2E
                m = jnp.maximum(jnp.max(sc, axis=-1, keepdims=True), sink)
                p = jnp.exp2(sc - m)
                denoms.append(jnp.sum(p, axis=-1, keepdims=True) + jnp.exp2(sink - m))
                probs.append(p.astype(BF16))
            o_all = _mm(jnp.concatenate(probs, axis=0), v_op)
            for i, h in enumerate(heads):
                res[h] = o_all[i * w:(i + 1) * w] / denoms[i]
        for pair in range(ATTN_HEADS // 2):
            cols = slice(pair * LANES, (pair + 1) * LANES)
            o_pair = jnp.where(lane_lo, res[2 * pair], res[2 * pair + 1])
            br_ref[1, rows, cols] = (o_pair * _silu(az_ref[rows, cols])).astype(BF16)


def _conv_silu(cq_ext, convw_ref, slab):
    sl = slice(slab * LANES, (slab + 1) * LANES)
    acc = None
    for back in range(GDN_CONV):
        tap = GDN_CONV - 1 - back
        term = cq_ext[CONV_TAIL - back:CONV_TAIL - back + TILE, sl] * convw_ref[tap:tap + 1, sl]
        acc = term if acc is None else acc + term
    return _silu(acc)


def _l2_norm(v):
    return v * lax.rsqrt(jnp.sum(v * v, axis=-1, keepdims=True) + NORM_EPS)


def _gdn_branch(cq_ext, cab_ref, cz_ref, convw_ref, alog_ref, dtb_ref, gng_ref, state, br_ref):
    c = GDN_BLOCK
    n_chunks = TILE // c
    heads = range(GDN_HEADS)
    x8 = cab_ref[...].T[0:8]
    sp_in = x8 + dtb_ref[...]
    softplus = jnp.maximum(sp_in, 0.0) + jnp.log(1.0 + jnp.exp(-jnp.abs(sp_in)))
    gc_t = (-LOG2E) * jnp.exp(alog_ref[...]) * softplus
    lane_in_chunk = lax.broadcasted_iota(jnp.int32, (8, TILE), 1) & (c - 1)
    k = 1
    while k < c:
        gc_t = gc_t + jnp.where(lane_in_chunk >= k, pltpu.roll(gc_t, k, 1), 0.0)
        k *= 2
    gl_t = jnp.concatenate([jnp.broadcast_to(gc_t[:, ch * c + c - 1:(ch + 1) * c], (8, c))
                            for ch in range(n_chunks)], axis=1)
    stack = jnp.concatenate([gc_t, _sigmoid(x8), jnp.exp2(gc_t), jnp.exp2(gl_t - gc_t), jnp.exp2(gl_t),
                             jnp.zeros((LANES - 40, TILE), F32)], axis=0)
    cols = stack.T
    gc_all, beta_all, eg_all, ekd_all, egl_all = (cols[:, 8 * i:8 * i + 8] for i in range(5))

    ri = lax.broadcasted_iota(jnp.int32, (TILE, TILE), 0)
    ci = lax.broadcasted_iota(jnp.int32, (TILE, TILE), 1)
    same_chunk = (ri & -c) == (ci & -c)
    causal = same_chunk & (ri >= ci)
    strict = same_chunk & (ri > ci)
    eye = (ri == ci).astype(F32)
    scale = GDN_HEAD_DIM ** -0.5

    a, qk, rhs, qd, kd_t = [], [], [], [], []
    for h in heads:
        qn = _l2_norm(_conv_silu(cq_ext, convw_ref, h)) * scale
        kn = _l2_norm(_conv_silu(cq_ext, convw_ref, GDN_HEADS + h))
        v = _conv_silu(cq_ext, convw_ref, 2 * GDN_HEADS + h)
        beta = beta_all[:, GDN_HEADS + h:GDN_HEADS + h + 1]
        eg = eg_all[:, h:h + 1]
        kb = kn * beta
        prod = _mm_tb(jnp.concatenate([kb, qn], axis=0), kn)
        decay = jnp.exp2(jnp.where(causal, gc_all[:, h:h + 1] - gc_t[h:h + 1, :], -jnp.inf))
        a.append(jnp.where(strict, prod[:TILE] * decay, 0.0).astype(BF16))
        qk.append((prod[TILE:] * decay).astype(BF16))
        rhs.append(jnp.concatenate([v * beta, kb * eg], axis=1).astype(BF16))
        qd.append(qn * eg)
        kd_t.append((kn * ekd_all[:, h:h + 1]).T.astype(BF16))

    rc = ri ^ ci
    zero = jnp.zeros((), BF16)
    inv = [eye - jnp.where(rc == 1, a_h, zero) for a_h in a]
    s = 2
    while s < c:
        link = (rc & -s) == s
        lnk = [jnp.where(link, a_h, zero) for a_h in a]
        if s < 8:
            tl = [_mm(i, l) for i, l in zip(inv, lnk)]
            inv = [i - _mm(t, i) for i, t in zip(inv, tl)]
        else:
            lower = [slice(b + s, b + 2 * s) for b in range(0, TILE, 2 * s)]
            upper = [slice(b, b + s) for b in range(0, TILE, 2 * s)]
            tl = [_mm(jnp.concatenate([i[r] for r in lower], axis=0), l) for i, l in zip(inv, lnk)]
            upd = [_mm(t, i) for i, t in zip(inv, tl)]
            inv = [jnp.concatenate([piece for n, (u, lo) in enumerate(zip(upper, lower))
                                    for piece in (i[u], i[lo] - d[n * s:(n + 1) * s])], axis=0)
                   for i, d in zip(inv, upd)]
        s *= 2
    uw = [_mm(i, r) for i, r in zip(inv, rhs)]

    eye_c = lax.broadcasted_iota(jnp.int32, (c, c), 0) == lax.broadcasted_iota(jnp.int32, (c, c), 1)
    lhs = [[None] * n_chunks for _ in heads]
    add = [[None] * n_chunks for _ in heads]
    for ch in range(n_chunks):
        rows = slice(ch * c, (ch + 1) * c)
        for h in heads:
            uw_c = uw[h][rows].astype(BF16)
            k_uw = _mm(kd_t[h][:, rows], uw_c)
            q_uw = _mm(qk[h][rows, rows], uw_c)
            dec = jnp.broadcast_to(egl_all[rows, h:h + 1], (c, LANES))
            lhs[h][ch] = jnp.concatenate([jnp.where(eye_c, dec, 0.0) - k_uw[:, LANES:],
                                          qd[h][rows] - q_uw[:, LANES:]], axis=0).astype(BF16)
            add[h][ch] = (k_uw[:, :LANES], q_uw[:, :LANES])
    s_cur = [state[h] for h in heads]
    outs = [[] for _ in heads]
    for ch in range(n_chunks):
        for h in heads:
            prod_s = _mm(lhs[h][ch], s_cur[h])
            outs[h].append(prod_s[c:] + add[h][ch][1])
            s_cur[h] = prod_s[:c] + add[h][ch][0]
    for h in heads:
        hs = slice(h * LANES, (h + 1) * LANES)
        state[h] = s_cur[h]
        o = jnp.concatenate(outs[h], axis=0)
        o = o * lax.rsqrt(jnp.mean(o * o, axis=-1, keepdims=True) + NORM_EPS) * gng_ref[...]
        br_ref[2, :, hs] = (o * _silu(cz_ref[:, hs])).astype(BF16)
    cq_ext[0:CONV_TAIL, :] = cq_ext[TILE:, :]


def _merge(final, x_ref, mg_ref, wbr_ref, wout_ref, fing_ref, br_ref, merged_ref, o_ref):
    half = D_MODEL // 2
    for c0 in range(0, D_MODEL, half):
        acc = None
        for n in range(3):
            up = jnp.dot(br_ref[n], wbr_ref[n, :, c0:c0 + half], preferred_element_type=F32)
            term = _sigmoid(mg_ref[:, n * D_MODEL + c0:n * D_MODEL + c0 + half]) * up
            acc = term if acc is None else acc + term
        merged_ref[:, c0:c0 + half] = acc.astype(BF16)
    y = x_ref[...] + jnp.dot(merged_ref[...], wout_ref[...], preferred_element_type=F32)
    if final:
        y = y * lax.rsqrt(jnp.mean(y * y, axis=-1, keepdims=True) + NORM_EPS) * fing_ref[...]
    o_ref[...] = y


def _layer_kernel(final, tiles_per_seq, x_ref, ng_ref, *refs):
    n_sec = len(_SECTIONS)
    names = [name for name, _ in _SECTIONS]
    w_refs = dict(zip(names, refs[:n_sec]))
    (poolw_ref, pscale_ref, sinks_ref, convw_ref, alog_ref, dtb_ref, gng_ref, wbr_ref, wout_ref, fing_ref,
     o_ref) = refs[n_sec:n_sec + 11]
    scratch = refs[n_sec + 11:]
    sec = dict(zip(names, scratch[:n_sec]))
    (pv_tail, kprev, vprev, state, hbuf, br) = scratch[n_sec:]
    merged = hbuf
    cq_ext = sec["cqkv"]
    s_idx = pl.program_id(0) % tiles_per_seq

    @pl.when(s_idx == 0)
    def _():
        pv_tail[...] = jnp.zeros_like(pv_tail)
        cq_ext[0:CONV_TAIL, :] = jnp.zeros((CONV_TAIL, 3 * BRANCH_WIDTH), F32)
        kprev[...] = jnp.zeros_like(kprev)
        vprev[...] = jnp.zeros_like(vprev)
        state[...] = jnp.zeros_like(state)

    _in_proj(x_ref, ng_ref, w_refs, hbuf, dict(sec, cqkv=cq_ext.at[pl.ds(CONV_TAIL, TILE)]))
    _pool_branch(s_idx, sec["pv"], sec["pz"], poolw_ref, pscale_ref, pv_tail, br)
    _attn_branch(s_idx == 0, sec["aq"], sec["akv"], sec["az"], sinks_ref, kprev, vprev, br)
    _gdn_branch(cq_ext, sec["cab"], sec["cz"], convw_ref, alog_ref, dtb_ref, gng_ref, state, br)
    _merge(final, x_ref, sec["mg"], wbr_ref, wout_ref, fing_ref, br, merged, o_ref)


def _layer(xt, ng_row, w_secs, poolw, pscale, sinks, convw, alog_col, dtb_col, gng_row, wbr, wout,
           fing_row, batch, final):
    t = xt.shape[0]
    n_tiles = t // TILE
    tiles_per_seq = n_tiles // batch
    const2 = lambda i: (0, 0)
    const3 = lambda i: (0, 0, 0)
    once = pl.Buffered(1)
    in_specs = [pl.BlockSpec((TILE, D_MODEL), lambda i: (i, 0)),
                pl.BlockSpec(ng_row.shape, const2)]
    in_specs += [pl.BlockSpec(w.shape, const2, pipeline_mode=once) for w in w_secs]
    in_specs += [
        pl.BlockSpec(poolw.shape, const3),
        pl.BlockSpec(pscale.shape, const2),
        pl.BlockSpec(memory_space=pltpu.SMEM),
        pl.BlockSpec(convw.shape, const2),
        pl.BlockSpec(alog_col.shape, const2),
        pl.BlockSpec(dtb_col.shape, const2),
        pl.BlockSpec(gng_row.shape, const2),
        pl.BlockSpec(wbr.shape, const3, pipeline_mode=once),
        pl.BlockSpec(wout.shape, const2, pipeline_mode=once),
        pl.BlockSpec(fing_row.shape, const2),
    ]
    sec_scratch = [pltpu.VMEM((TILE + (CONV_TAIL if name == "cqkv" else 0), w), F32) for name, w in _SECTIONS]
    return pl.pallas_call(
        functools.partial(_layer_kernel, final, tiles_per_seq),
        grid=(n_tiles,),
        in_specs=in_specs,
        out_specs=pl.BlockSpec((TILE, D_MODEL), lambda i: (i, 0)),
        out_shape=jax.ShapeDtypeStruct((t, D_MODEL), F32),
        scratch_shapes=sec_scratch + [
            pltpu.VMEM((POOL_TAIL, BRANCH_WIDTH), F32),
            pltpu.VMEM((ATTN_WINDOW, LANES), F32),
            pltpu.VMEM((ATTN_WINDOW, LANES), F32),
            pltpu.VMEM((GDN_HEADS, GDN_HEAD_DIM, GDN_HEAD_DIM), F32),
            pltpu.VMEM((TILE, D_MODEL), BF16),
            pltpu.VMEM((3, TILE, BRANCH_WIDTH), BF16),
        ],
        compiler_params=pltpu.CompilerParams(dimension_semantics=("arbitrary",),
                                             vmem_limit_bytes=VMEM_LIMIT),
        name="layer_final" if final else "layer",
    )(xt, ng_row, *w_secs, poolw, pscale, sinks, convw, alog_col, dtb_col, gng_row, wbr, wout, fing_row)


def _w_in_sections(w):
    parts = []
    for name, width in _SECTIONS:
        lo, hi = _COLS[name]
        part = w[:, lo:hi].astype(BF16)
        if hi - lo < width:
            part = jnp.pad(part, ((0, 0), (0, width - (hi - lo))))
        parts.append(part)
    return parts


def _head_col(v):
    return jnp.pad(v.astype(F32), (0, 8 - v.shape[0])).reshape(8, 1)


def _pool_pairs(w):
    z = jnp.zeros_like(w[0])
    return jnp.stack([jnp.block([[w[2 * p], z], [z, w[2 * p + 1]]]) for p in range(len(POOL_WINDOWS) // 2)]
                     ).astype(BF16)


def kernel(x, norm_g, w_in, pool_w, pool_scale, attn_sinks, conv_w, a_log, dt_bias, gdn_norm_g, w_branch,
           w_out, final_norm_g):
    batch, seq, d = x.shape
    assert d == D_MODEL and seq % TILE == 0
    xt = x.reshape(batch * seq, d)
    for l in range(DEPTH):
        xt = _layer(xt, norm_g[l].reshape(1, d), _w_in_sections(w_in[l]),
                    _pool_pairs(pool_w[l]), pool_scale[l].reshape(1, BRANCH_WIDTH), attn_sinks[l],
                    conv_w[l], _head_col(a_log[l]), _head_col(dt_bias[l]),
                    gdn_norm_g[l].reshape(1, GDN_HEAD_DIM),
                    w_branch[l].astype(BF16), w_out[l].astype(BF16), final_norm_g.reshape(1, d),
                    batch, final=(l == DEPTH - 1))
    return xt.reshape(batch, seq, d)
```

```python
import functools

import jax
import jax.numpy as jnp
from jax import lax
from jax.experimental import pallas as pl
from jax.experimental.pallas import tpu as pltpu

F32 = jnp.float32
BF16 = jnp.bfloat16

D_MODEL = 1024
DEPTH = 2
BRANCH_WIDTH = 512
POOL_WINDOWS = (2, 4, 8, 16)
POOL_CH = 128
ATTN_HEAD_DIM = 64
ATTN_HEADS = 8
ATTN_GROUP = 4
ATTN_WINDOW = 128
GDN_HEAD_DIM = 128
GDN_HEADS = 4
GDN_CONV = 4
GDN_BLOCK = 128
NORM_EPS = 1e-6
LOG2E = 1.4426950408889634

_COLS = dict(pv=(0, 512), pz=(512, 1024), aq=(1024, 1536), akv=(1536, 1792),
             az=(1792, 2304), cqkv=(2304, 3840), cab=(3840, 3848), cz=(3848, 4360), mg=(4360, 7432))
_SECTIONS = (("mg", 3072), ("cqkv", 1536), ("cz", 512), ("cab", 128), ("pv", 512), ("pz", 512),
             ("aq", 512), ("az", 512), ("akv", 256))

LANES = 128
TILE = 256
PROJ_CHUNK = 512
POOL_TAIL = 16
CONV_TAIL = 8
VMEM_LIMIT = 56 * 1024 * 1024


def _sigmoid(v):
    return 0.5 * jnp.tanh(0.5 * v) + 0.5


def _silu(v):
    hv = 0.5 * v
    return hv * (jnp.tanh(hv) + 1.0)


def _mm(a, b):
    return jnp.dot(a.astype(BF16), b.astype(BF16), preferred_element_type=F32)


def _mm_tb(a, b):
    return lax.dot_general(a.astype(BF16), b.astype(BF16), (((1,), (1,)), ((), ())),
                           preferred_element_type=F32)


def _normalise(x_ref, g_ref, h_ref):
    x = x_ref[...]
    ms = jnp.mean(x * x, axis=-1, keepdims=True)
    h_ref[...] = (x * lax.rsqrt(ms + NORM_EPS) * g_ref[...]).astype(BF16)


def _project(names, w_refs, h_ref, dst):
    for name, width in _SECTIONS:
        if name in names:
            for c in range(0, width, PROJ_CHUNK):
                cw = min(PROJ_CHUNK, width - c)
                dst[name][:, c:c + cw] = jnp.dot(h_ref[...], w_refs[name][:, c:c + cw],
                                                 preferred_element_type=F32)


def _pool_branch(s_idx, pv_ref, pz_ref, poolw_ref, pscale_ref, pv_tail, br_ref):
    pos = (s_idx * TILE + 1 + lax.broadcasted_iota(jnp.int32, (TILE, LANES), 0)).astype(F32)
    pooled = []
    for gi, win in enumerate(POOL_WINDOWS):
        sl = slice(gi * POOL_CH, (gi + 1) * POOL_CH)
        cur = pv_ref[:, sl]
        acc = jnp.concatenate([pv_tail[:, sl], cur], axis=0)
        k = 1
        while k < win:
            acc = acc + pltpu.roll(acc, k, 0)
            k *= 2
        pooled.append((acc[POOL_TAIL:] / jnp.minimum(pos, float(win)) - cur).astype(BF16))
        pv_tail[:, sl] = cur[TILE - POOL_TAIL:]
        if gi % 2 == 1:
            sl2 = slice((gi - 1) * POOL_CH, (gi + 1) * POOL_CH)
            mixed = _mm(jnp.concatenate(pooled[-2:], axis=1), poolw_ref[gi // 2]) * pscale_ref[:, sl2]
            br_ref[0, :, sl2] = (mixed * _silu(pz_ref[:, sl2])).astype(BF16)


def _attn_branch(first_in_seq, aq_ref, akv_ref, az_ref, sinks_ref, kprev, vprev, br_ref):
    w = ATTN_WINDOW
    k_all = jnp.concatenate([kprev[...], akv_ref[:, :LANES]], axis=0)
    v_all = jnp.concatenate([vprev[...], akv_ref[:, LANES:]], axis=0)
    kprev[...] = akv_ref[TILE - w:, :LANES]
    vprev[...] = akv_ref[TILE - w:, LANES:]
    k_sw = pltpu.roll(k_all, ATTN_HEAD_DIM, 1).astype(BF16)
    v_sw = pltpu.roll(v_all, ATTN_HEAD_DIM, 1).astype(BF16)
    k_all = k_all.astype(BF16)
    v_all = v_all.astype(BF16)

    qi = lax.broadcasted_iota(jnp.int32, (w, 2 * w), 0)
    kj = lax.broadcasted_iota(jnp.int32, (w, 2 * w), 1)
    dist = qi + w - kj
    in_window = (dist >= 0) & (dist < w)
    distf = dist.astype(F32)
    lane_lo = lax.broadcasted_iota(jnp.int32, (w, LANES), 1) < ATTN_HEAD_DIM
    scale = ATTN_HEAD_DIM ** -0.5 * LOG2E

    for blk in range(TILE // w):
        rows = slice(blk * w, (blk + 1) * w)
        keys = slice(blk * w, blk * w + 2 * w)
        valid = in_window
        if blk == 0:
            valid = valid & ((kj >= w) | jnp.logical_not(first_in_seq))
        res = [None] * ATTN_HEADS
        for same in (True, False):
            heads = [h for h in range(ATTN_HEADS) if ((h // ATTN_GROUP) == (h % 2)) == same]
            k_op = (k_all if same else k_sw)[keys]
            v_op = (v_all if same else v_sw)[keys]
            q_stack = jnp.concatenate(
                [jnp.where(lane_lo if h % 2 == 0 else ~lane_lo,
                           aq_ref[rows, (h // 2) * LANES:(h // 2 + 1) * LANES] * scale, 0.0).astype(BF16)
                 for h in heads], axis=0)
            sc_all = _mm_tb(q_stack, k_op)
            probs, denoms = [], []
            for i, h in enumerate(heads):
                slope = 2.0 ** (-(h + 1)) * LOG2E
                sc = jnp.where(valid, sc_all[i * w:(i + 1) * w] - slope * distf, -jnp.inf)
                sink = sinks_ref[h] * LOG2E
                m = jnp.maximum(jnp.max(sc, axis=-1, keepdims=True), sink)
                p = jnp.exp2(sc - m)
                denoms.append(jnp.sum(p, axis=-1, keepdims=True) + jnp.exp2(sink - m))
                probs.append(p.astype(BF16))
            o_all = _mm(jnp.concatenate(probs, axis=0), v_op)
            for i, h in enumerate(heads):
                res[h] = o_all[i * w:(i + 1) * w] / denoms[i]
        for pair in range(ATTN_HEADS // 2):
            cols = slice(pair * LANES, (pair + 1) * LANES)
            o_pair = jnp.where(lane_lo, res[2 * pair], res[2 * pair + 1])
            br_ref[1, rows, cols] = (o_pair * _silu(az_ref[rows, cols])).astype(BF16)


def _conv_silu(cq_ext, convw_ref, slab):
    sl = slice(slab * LANES, (slab + 1) * LANES)
    acc = None
    for back in range(GDN_CONV):
        tap = GDN_CONV - 1 - back
        term = cq_ext[CONV_TAIL - back:CONV_TAIL - back + TILE, sl] * convw_ref[tap:tap + 1, sl]
        acc = term if acc is None else acc + term
    return _silu(acc)


def _l2_norm(v):
    return v * lax.rsqrt(jnp.sum(v * v, axis=-1, keepdims=True) + NORM_EPS)


def _gdn_branch(cq_ext, cab_ref, cz_ref, convw_ref, alog_ref, dtb_ref, gng_ref, state, br_ref):
    c = GDN_BLOCK
    n_chunks = TILE // c
    heads = range(GDN_HEADS)
    x8 = cab_ref[...].T[0:8]
    sp_in = x8 + dtb_ref[...]
    softplus = jnp.maximum(sp_in, 0.0) + jnp.log(1.0 + jnp.exp(-jnp.abs(sp_in)))
    gc_t = (-LOG2E) * jnp.exp(alog_ref[...]) * softplus
    lane_in_chunk = lax.broadcasted_iota(jnp.int32, (8, TILE), 1) & (c - 1)
    k = 1
    while k < c:
        gc_t = gc_t + jnp.where(lane_in_chunk >= k, pltpu.roll(gc_t, k, 1), 0.0)
        k *= 2
    gl_t = jnp.concatenate([jnp.broadcast_to(gc_t[:, ch * c + c - 1:(ch + 1) * c], (8, c))
                            for ch in range(n_chunks)], axis=1)
    stack = jnp.concatenate([gc_t, _sigmoid(x8), jnp.exp2(gc_t), jnp.exp2(gl_t - gc_t), jnp.exp2(gl_t),
                             jnp.zeros((LANES - 40, TILE), F32)], axis=0)
    cols = stack.T
    gc_all, beta_all, eg_all, ekd_all, egl_all = (cols[:, 8 * i:8 * i + 8] for i in range(5))

    ri = lax.broadcasted_iota(jnp.int32, (TILE, TILE), 0)
    ci = lax.broadcasted_iota(jnp.int32, (TILE, TILE), 1)
    same_chunk = (ri & -c) == (ci & -c)
    causal = same_chunk & (ri >= ci)
    strict = same_chunk & (ri > ci)
    eye = (ri == ci).astype(F32)
    scale = GDN_HEAD_DIM ** -0.5

    a, qk, rhs, qd, kd_t = [], [], [], [], []
    for h in heads:
        qn = _l2_norm(_conv_silu(cq_ext, convw_ref, h)) * scale
        kn = _l2_norm(_conv_silu(cq_ext, convw_ref, GDN_HEADS + h))
        v = _conv_silu(cq_ext, convw_ref, 2 * GDN_HEADS + h)
        beta = beta_all[:, GDN_HEADS + h:GDN_HEADS + h + 1]
        eg = eg_all[:, h:h + 1]
        kb = kn * beta
        prod = _mm_tb(jnp.concatenate([kb, qn], axis=0), kn)
        decay = jnp.exp2(jnp.where(causal, gc_all[:, h:h + 1] - gc_t[h:h + 1, :], -jnp.inf))
        a.append(jnp.where(strict, prod[:TILE] * decay, 0.0).astype(BF16))
        qk.append((prod[TILE:] * decay).astype(BF16))
        rhs.append(jnp.concatenate([v * beta, kb * eg], axis=1).astype(BF16))
        qd.append(qn * eg)
        kd_t.append((kn * ekd_all[:, h:h + 1]).T.astype(BF16))

    rc = ri ^ ci
    zero = jnp.zeros((), BF16)
    inv = [eye - jnp.where(rc == 1, a_h, zero) for a_h in a]
    s = 2
    while s < c:
        link = (rc & -s) == s
        lnk = [jnp.where(link, a_h, zero) for a_h in a]
        if s < 8:
            tl = [_mm(i, l) for i, l in zip(inv, lnk)]
            inv = [i - _mm(t, i) for i, t in zip(inv, tl)]
        else:
            lower = [slice(b + s, b + 2 * s) for b in range(0, TILE, 2 * s)]
            upper = [slice(b, b + s) for b in range(0, TILE, 2 * s)]
            tl = [_mm(jnp.concatenate([i[r] for r in lower], axis=0), l) for i, l in zip(inv, lnk)]
            upd = [_mm(t, i) for i, t in zip(inv, tl)]
            inv = [jnp.concatenate([piece for n, (u, lo) in enumerate(zip(upper, lower))
                                    for piece in (i[u], i[lo] - d[n * s:(n + 1) * s])], axis=0)
                   for i, d in zip(inv, upd)]
        s *= 2
    uw = [_mm(i, r) for i, r in zip(inv, rhs)]

    eye_c = lax.broadcasted_iota(jnp.int32, (c, c), 0) == lax.broadcasted_iota(jnp.int32, (c, c), 1)
    lhs = [[None] * n_chunks for _ in heads]
    add = [[None] * n_chunks for _ in heads]
    for ch in range(n_chunks):
        rows = slice(ch * c, (ch + 1) * c)
        for h in heads:
            uw_c = uw[h][rows].astype(BF16)
            k_uw = _mm(kd_t[h][:, rows], uw_c)
            q_uw = _mm(qk[h][rows, rows], uw_c)
            dec = jnp.broadcast_to(egl_all[rows, h:h + 1], (c, LANES))
            lhs[h][ch] = jnp.concatenate([jnp.where(eye_c, dec, 0.0) - k_uw[:, LANES:],
                                          qd[h][rows] - q_uw[:, LANES:]], axis=0).astype(BF16)
            add[h][ch] = (k_uw[:, :LANES], q_uw[:, :LANES])
    s_cur = [state[h] for h in heads]
    outs = [[] for _ in heads]
    for ch in range(n_chunks):
        for h in heads:
            prod_s = _mm(lhs[h][ch], s_cur[h])
            outs[h].append(prod_s[c:] + add[h][ch][1])
            s_cur[h] = prod_s[:c] + add[h][ch][0]
    for h in heads:
        hs = slice(h * LANES, (h + 1) * LANES)
        state[h] = s_cur[h]
        o = jnp.concatenate(outs[h], axis=0)
        o = o * lax.rsqrt(jnp.mean(o * o, axis=-1, keepdims=True) + NORM_EPS) * gng_ref[...]
        br_ref[2, :, hs] = (o * _silu(cz_ref[:, hs])).astype(BF16)
    cq_ext[0:CONV_TAIL, :] = cq_ext[TILE:, :]


def _merge(final, x_ref, mg_ref, wbr_ref, wout_ref, fing_ref, br_ref, merged_ref, o_ref):
    half = D_MODEL // 2
    for c0 in range(0, D_MODEL, half):
        acc = None
        for n in range(3):
            up = jnp.dot(br_ref[n], wbr_ref[n, :, c0:c0 + half], preferred_element_type=F32)
            term = _sigmoid(mg_ref[:, n * D_MODEL + c0:n * D_MODEL + c0 + half]) * up
            acc = term if acc is None else acc + term
        merged_ref[:, c0:c0 + half] = acc.astype(BF16)
    y = x_ref[...] + jnp.dot(merged_ref[...], wout_ref[...], preferred_element_type=F32)
    if final:
        y = y * lax.rsqrt(jnp.mean(y * y, axis=-1, keepdims=True) + NORM_EPS) * fing_ref[...]
    o_ref[...] = y


def _layer_kernel(final, tiles_per_seq, x_ref, ng_ref, *refs):
    n_sec = len(_SECTIONS)
    names = [name for name, _ in _SECTIONS]
    w_refs = dict(zip(names, refs[:n_sec]))
    (poolw_ref, pscale_ref, sinks_ref, convw_ref, alog_ref, dtb_ref, gng_ref, wbr_ref, wout_ref, fing_ref,
     o_ref) = refs[n_sec:n_sec + 11]
    scratch = refs[n_sec + 11:]
    sec = dict(zip(names, scratch[:n_sec]))
    (pv_tail, kprev, vprev, state, hbuf, br) = scratch[n_sec:]
    merged = hbuf
    cq_ext = sec["cqkv"]
    s_idx = pl.program_id(0) % tiles_per_seq

    @pl.when(s_idx == 0)
    def _():
        pv_tail[...] = jnp.zeros_like(pv_tail)
        cq_ext[0:CONV_TAIL, :] = jnp.zeros((CONV_TAIL, 3 * BRANCH_WIDTH), F32)
        kprev[...] = jnp.zeros_like(kprev)
        vprev[...] = jnp.zeros_like(vprev)
        state[...] = jnp.zeros_like(state)

    dst = dict(sec, cqkv=cq_ext.at[pl.ds(CONV_TAIL, TILE)])
    _normalise(x_ref, ng_ref, hbuf)
    _project(("cqkv", "cab", "pv", "pz", "aq", "az", "akv"), w_refs, hbuf, dst)
    _pool_branch(s_idx, sec["pv"], sec["pz"], poolw_ref, pscale_ref, pv_tail, br)
    _attn_branch(s_idx == 0, sec["aq"], sec["akv"], sec["az"], sinks_ref, kprev, vprev, br)
    _project(("cz", "mg"), w_refs, hbuf, dst)
    _gdn_branch(cq_ext, sec["cab"], sec["cz"], convw_ref, alog_ref, dtb_ref, gng_ref, state, br)
    _merge(final, x_ref, sec["mg"], wbr_ref, wout_ref, fing_ref, br, merged, o_ref)


def _layer(xt, ng_row, w_secs, poolw, pscale, sinks, convw, alog_col, dtb_col, gng_row, wbr, wout,
           fing_row, batch, final):
    t = xt.shape[0]
    n_tiles = t // TILE
    tiles_per_seq = n_tiles // batch
    const2 = lambda i: (0, 0)
    const3 = lambda i: (0, 0, 0)
    once = pl.Buffered(1)
    in_specs = [pl.BlockSpec((TILE, D_MODEL), lambda i: (i, 0)),
                pl.BlockSpec(ng_row.shape, const2)]
    in_specs += [pl.BlockSpec(w.shape, const2, pipeline_mode=once) for w in w_secs]
    in_specs += [
        pl.BlockSpec(poolw.shape, const3),
        pl.BlockSpec(pscale.shape, const2),
        pl.BlockSpec(memory_space=pltpu.SMEM),
        pl.BlockSpec(convw.shape, const2),
        pl.BlockSpec(alog_col.shape, const2),
        pl.BlockSpec(dtb_col.shape, const2),
        pl.BlockSpec(gng_row.shape, const2),
        pl.BlockSpec(wbr.shape, const3, pipeline_mode=once),
        pl.BlockSpec(wout.shape, const2, pipeline_mode=once),
        pl.BlockSpec(fing_row.shape, const2),
    ]
    sec_scratch = [pltpu.VMEM((TILE + (CONV_TAIL if name == "cqkv" else 0), w), F32) for name, w in _SECTIONS]
    return pl.pallas_call(
        functools.partial(_layer_kernel, final, tiles_per_seq),
        grid=(n_tiles,),
        in_specs=in_specs,
        out_specs=pl.BlockSpec((TILE, D_MODEL), lambda i: (i, 0)),
        out_shape=jax.ShapeDtypeStruct((t, D_MODEL), F32),
        scratch_shapes=sec_scratch + [
            pltpu.VMEM((POOL_TAIL, BRANCH_WIDTH), F32),
            pltpu.VMEM((ATTN_WINDOW, LANES), F32),
            pltpu.VMEM((ATTN_WINDOW, LANES), F32),
            pltpu.VMEM((GDN_HEADS, GDN_HEAD_DIM, GDN_HEAD_DIM), F32),
            pltpu.VMEM((TILE, D_MODEL), BF16),
            pltpu.VMEM((3, TILE, BRANCH_WIDTH), BF16),
        ],
        compiler_params=pltpu.CompilerParams(dimension_semantics=("arbitrary",),
                                             vmem_limit_bytes=VMEM_LIMIT),
        name="layer_final" if final else "layer",
    )(xt, ng_row, *w_secs, poolw, pscale, sinks, convw, alog_col, dtb_col, gng_row, wbr, wout, fing_row)


def _w_in_sections(w):
    parts = []
    for name, width in _SECTIONS:
        lo, hi = _COLS[name]
        part = w[:, lo:hi].astype(BF16)
        if hi - lo < width:
            part = jnp.pad(part, ((0, 0), (0, width - (hi - lo))))
        parts.append(part)
    return parts


def _head_col(v):
    return jnp.pad(v.astype(F32), (0, 8 - v.shape[0])).reshape(8, 1)


def _pool_pairs(w):
    z = jnp.zeros_like(w[0])
    return jnp.stack([jnp.block([[w[2 * p], z], [z, w[2 * p + 1]]]) for p in range(len(POOL_WINDOWS) // 2)]
                     ).astype(BF16)


def kernel(x, norm_g, w_in, pool_w, pool_scale, attn_sinks, conv_w, a_log, dt_bias, gdn_norm_g, w_branch,
           w_out, final_norm_g):
    batch, seq, d = x.shape
    assert d == D_MODEL and seq % TILE == 0
    xt = x.reshape(batch * seq, d)
    for l in range(DEPTH):
        xt = _layer(xt, norm_g[l].reshape(1, d), _w_in_sections(w_in[l]),
                    _pool_pairs(pool_w[l]), pool_scale[l].reshape(1, BRANCH_WIDTH), attn_sinks[l],
                    conv_w[l], _head_col(a_log[l]), _head_col(dt_bias[l]),
                    gdn_norm_g[l].reshape(1, GDN_HEAD_DIM),
                    w_branch[l].astype(BF16), w_out[l].astype(BF16), final_norm_g.reshape(1, d),
                    batch, final=(l == DEPTH - 1))
    return xt.reshape(batch, seq, d)
```

```python
import functools

import jax
import jax.numpy as jnp
from jax import lax
from jax.experimental import pallas as pl
from jax.experimental.pallas import tpu as pltpu

F32 = jnp.float32
BF16 = jnp.bfloat16

D_MODEL = 1024
DEPTH = 2
BRANCH_WIDTH = 512
POOL_WINDOWS = (2, 4, 8, 16)
POOL_CH = 128
ATTN_HEAD_DIM = 64
ATTN_HEADS = 8
ATTN_GROUP = 4
ATTN_WINDOW = 128
GDN_HEAD_DIM = 128
GDN_HEADS = 4
GDN_CONV = 4
GDN_BLOCK = 128
NORM_EPS = 1e-6
LOG2E = 1.4426950408889634

_COLS = dict(pv=(0, 512), pz=(512, 1024), aq=(1024, 1536), akv=(1536, 1792),
             az=(1792, 2304), cqkv=(2304, 3840), cab=(3840, 3848), cz=(3848, 4360), mg=(4360, 7432))
_SECTIONS = (("mg", 3072), ("cqkv", 1536), ("cz", 512), ("cab", 128), ("pv", 512), ("pz", 512),
             ("aq", 512), ("az", 512), ("akv", 256))

LANES = 128
TILE = 256
STEP_TILES = 2
PROJ_CHUNK = 512
POOL_TAIL = 16
CONV_TAIL = 8
VMEM_LIMIT = 56 * 1024 * 1024


def _sigmoid(v):
    return 0.5 * jnp.tanh(0.5 * v) + 0.5


def _silu(v):
    hv = 0.5 * v
    return hv * (jnp.tanh(hv) + 1.0)


def _mm(a, b):
    return jnp.dot(a.astype(BF16), b.astype(BF16), preferred_element_type=F32)


def _mm_tb(a, b):
    return lax.dot_general(a.astype(BF16), b.astype(BF16), (((1,), (1,)), ((), ())),
                           preferred_element_type=F32)


def _normalise(x_ref, g_ref, h_ref):
    x = x_ref[...]
    ms = jnp.mean(x * x, axis=-1, keepdims=True)
    h_ref[...] = (x * lax.rsqrt(ms + NORM_EPS) * g_ref[...]).astype(BF16)


def _project(names, w_refs, h_ref, dst):
    widths = dict(_SECTIONS)
    for name in names:
        for c in range(0, widths[name], PROJ_CHUNK):
            cw = min(PROJ_CHUNK, widths[name] - c)
            dst[name][:, c:c + cw] = jnp.dot(h_ref[...], w_refs[name][:, c:c + cw],
                                             preferred_element_type=F32)


def _pool_branch(s_idx, pv_ref, pz_ref, poolw_ref, pscale_ref, pv_tail, br_ref):
    pos = (s_idx * TILE + 1 + lax.broadcasted_iota(jnp.int32, (TILE, LANES), 0)).astype(F32)
    pooled = []
    for gi, win in enumerate(POOL_WINDOWS):
        sl = slice(gi * POOL_CH, (gi + 1) * POOL_CH)
        cur = pv_ref[:, sl]
        acc = jnp.concatenate([pv_tail[:, sl], cur], axis=0)
        k = 1
        while k < win:
            acc = acc + pltpu.roll(acc, k, 0)
            k *= 2
        pooled.append((acc[POOL_TAIL:] / jnp.minimum(pos, float(win)) - cur).astype(BF16))
        pv_tail[:, sl] = cur[TILE - POOL_TAIL:]
        if gi % 2 == 1:
            sl2 = slice((gi - 1) * POOL_CH, (gi + 1) * POOL_CH)
            mixed = _mm(jnp.concatenate(pooled[-2:], axis=1), poolw_ref[gi // 2]) * pscale_ref[:, sl2]
            br_ref[0, :, sl2] = (mixed * _silu(pz_ref[:, sl2])).astype(BF16)


def _attn_branch(first_in_seq, aq_ref, akv_ref, az_ref, sinks_ref, kprev, vprev, br_ref):
    w = ATTN_WINDOW
    k_all = jnp.concatenate([kprev[...], akv_ref[:, :LANES]], axis=0)
    v_all = jnp.concatenate([vprev[...], akv_ref[:, LANES:]], axis=0)
    kprev[...] = akv_ref[TILE - w:, :LANES]
    vprev[...] = akv_ref[TILE - w:, LANES:]
    k_sw = pltpu.roll(k_all, ATTN_HEAD_DIM, 1).astype(BF16)
    v_sw = pltpu.roll(v_all, ATTN_HEAD_DIM, 1).astype(BF16)
    k_all = k_all.astype(BF16)
    v_all = v_all.astype(BF16)

    qi = lax.broadcasted_iota(jnp.int32, (w, 2 * w), 0)
    kj = lax.broadcasted_iota(jnp.int32, (w, 2 * w), 1)
    dist = qi + w - kj
    in_window = (dist >= 0) & (dist < w)
    distf = dist.astype(F32)
    lane_lo = lax.broadcasted_iota(jnp.int32, (w, LANES), 1) < ATTN_HEAD_DIM
    scale = ATTN_HEAD_DIM ** -0.5 * LOG2E

    for blk in range(TILE // w):
        rows = slice(blk * w, (blk + 1) * w)
        keys = slice(blk * w, blk * w + 2 * w)
        valid = in_window
        if blk == 0:
            valid = valid & ((kj >= w) | jnp.logical_not(first_in_seq))
        res = [None] * ATTN_HEADS
        for same in (True, False):
            heads = [h for h in range(ATTN_HEADS) if ((h // ATTN_GROUP) == (h % 2)) == same]
            k_op = (k_all if same else k_sw)[keys]
            v_op = (v_all if same else v_sw)[keys]
            q_stack = jnp.concatenate(
                [jnp.where(lane_lo if h % 2 == 0 else ~lane_lo,
                           aq_ref[rows, (h // 2) * LANES:(h // 2 + 1) * LANES] * scale, 0.0).astype(BF16)
                 for h in heads], axis=0)
            sc_all = _mm_tb(q_stack, k_op)
            probs, denoms = [], []
            for i, h in enumerate(heads):
                slope = 2.0 ** (-(h + 1)) * LOG2E
                sc = jnp.where(valid, sc_all[i * w:(i + 1) * w] - slope * distf, -jnp.inf)
                sink = sinks_ref[h] * LOG2E
                m = jnp.maximum(jnp.max(sc, axis=-1, keepdims=True), sink)
                p = jnp.exp2(sc - m)
                denoms.append(jnp.sum(p, axis=-1, keepdims=True) + jnp.exp2(sink - m))
                probs.append(p.astype(BF16))
            o_all = _mm(jnp.concatenate(probs, axis=0), v_op)
            for i, h in enumerate(heads):
                res[h] = o_all[i * w:(i + 1) * w] / denoms[i]
        for pair in range(ATTN_HEADS // 2):
            cols = slice(pair * LANES, (pair + 1) * LANES)
            o_pair = jnp.where(lane_lo, res[2 * pair], res[2 * pair + 1])
            br_ref[1, rows, cols] = (o_pair * _silu(az_ref[rows, cols])).astype(BF16)


def _conv_silu(cq_ext, convw_ref, slab):
    sl = slice(slab * LANES, (slab + 1) * LANES)
    acc = None
    for back in range(GDN_CONV):
        tap = GDN_CONV - 1 - back
        term = cq_ext[CONV_TAIL - back:CONV_TAIL - back + TILE, sl] * convw_ref[tap:tap + 1, sl]
        acc = term if acc is None else acc + term
    return _silu(acc)


def _l2_norm(v):
    return v * lax.rsqrt(jnp.sum(v * v, axis=-1, keepdims=True) + NORM_EPS)


def _gdn_branch(cq_ext, cab_ref, cz_ref, convw_ref, alog_ref, dtb_ref, gng_ref, state, br_ref):
    c = GDN_BLOCK
    n_chunks = TILE // c
    heads = range(GDN_HEADS)
    x8 = cab_ref[...].T[0:8]
    sp_in = x8 + dtb_ref[...]
    softplus = jnp.maximum(sp_in, 0.0) + jnp.log(1.0 + jnp.exp(-jnp.abs(sp_in)))
    gc_t = (-LOG2E) * jnp.exp(alog_ref[...]) * softplus
    lane_in_chunk = lax.broadcasted_iota(jnp.int32, (8, TILE), 1) & (c - 1)
    k = 1
    while k < c:
        gc_t = gc_t + jnp.where(lane_in_chunk >= k, pltpu.roll(gc_t, k, 1), 0.0)
        k *= 2
    gl_t = jnp.concatenate([jnp.broadcast_to(gc_t[:, ch * c + c - 1:(ch + 1) * c], (8, c))
                            for ch in range(n_chunks)], axis=1)
    stack = jnp.concatenate([gc_t, _sigmoid(x8), jnp.exp2(gc_t), jnp.exp2(gl_t - gc_t), jnp.exp2(gl_t),
                             jnp.zeros((LANES - 40, TILE), F32)], axis=0)
    cols = stack.T
    gc_all, beta_all, eg_all, ekd_all, egl_all = (cols[:, 8 * i:8 * i + 8] for i in range(5))

    ri = lax.broadcasted_iota(jnp.int32, (TILE, TILE), 0)
    ci = lax.broadcasted_iota(jnp.int32, (TILE, TILE), 1)
    same_chunk = (ri & -c) == (ci & -c)
    causal = same_chunk & (ri >= ci)
    strict = same_chunk & (ri > ci)
    eye = (ri == ci).astype(F32)
    scale = GDN_HEAD_DIM ** -0.5

    a, qk, rhs, qd, kd_t = [], [], [], [], []
    for h in heads:
        qn = _l2_norm(_conv_silu(cq_ext, convw_ref, h)) * scale
        kn = _l2_norm(_conv_silu(cq_ext, convw_ref, GDN_HEADS + h))
        v = _conv_silu(cq_ext, convw_ref, 2 * GDN_HEADS + h)
        beta = beta_all[:, GDN_HEADS + h:GDN_HEADS + h + 1]
        eg = eg_all[:, h:h + 1]
        kb = kn * beta
        prod = _mm_tb(jnp.concatenate([kb, qn], axis=0), kn)
        decay = jnp.exp2(jnp.where(causal, gc_all[:, h:h + 1] - gc_t[h:h + 1, :], -jnp.inf))
        a.append(jnp.where(strict, prod[:TILE] * decay, 0.0).astype(BF16))
        qk.append((prod[TILE:] * decay).astype(BF16))
        rhs.append(jnp.concatenate([v * beta, kb * eg], axis=1).astype(BF16))
        qd.append(qn * eg)
        kd_t.append((kn * ekd_all[:, h:h + 1]).T.astype(BF16))

    rc = ri ^ ci
    zero = jnp.zeros((), BF16)
    inv = [eye - jnp.where(rc == 1, a_h, zero) for a_h in a]
    s = 2
    while s < c:
        link = (rc & -s) == s
        lnk = [jnp.where(link, a_h, zero) for a_h in a]
        if s < 8:
            tl = [_mm(i, l) for i, l in zip(inv, lnk)]
            inv = [i - _mm(t, i) for i, t in zip(inv, tl)]
        else:
            lower = [slice(b + s, b + 2 * s) for b in range(0, TILE, 2 * s)]
            upper = [slice(b, b + s) for b in range(0, TILE, 2 * s)]
            tl = [_mm(jnp.concatenate([i[r] for r in lower], axis=0), l) for i, l in zip(inv, lnk)]
            upd = [_mm(t, i) for i, t in zip(inv, tl)]
            inv = [jnp.concatenate([piece for n, (u, lo) in enumerate(zip(upper, lower))
                                    for piece in (i[u], i[lo] - d[n * s:(n + 1) * s])], axis=0)
                   for i, d in zip(inv, upd)]
        s *= 2
    uw = [_mm(i, r) for i, r in zip(inv, rhs)]

    eye_c = lax.broadcasted_iota(jnp.int32, (c, c), 0) == lax.broadcasted_iota(jnp.int32, (c, c), 1)
    lhs = [[None] * n_chunks for _ in heads]
    add = [[None] * n_chunks for _ in heads]
    for ch in range(n_chunks):
        rows = slice(ch * c, (ch + 1) * c)
        for h in heads:
            uw_c = uw[h][rows].astype(BF16)
            k_uw = _mm(kd_t[h][:, rows], uw_c)
            q_uw = _mm(qk[h][rows, rows], uw_c)
            dec = jnp.broadcast_to(egl_all[rows, h:h + 1], (c, LANES))
            lhs[h][ch] = jnp.concatenate([jnp.where(eye_c, dec, 0.0) - k_uw[:, LANES:],
                                          qd[h][rows] - q_uw[:, LANES:]], axis=0).astype(BF16)
            add[h][ch] = (k_uw[:, :LANES], q_uw[:, :LANES])
    s_cur = [state[h] for h in heads]
    outs = [[] for _ in heads]
    for ch in range(n_chunks):
        for h in heads:
            prod_s = _mm(lhs[h][ch], s_cur[h])
            outs[h].append(prod_s[c:] + add[h][ch][1])
            s_cur[h] = prod_s[:c] + add[h][ch][0]
    for h in heads:
        hs = slice(h * LANES, (h + 1) * LANES)
        state[h] = s_cur[h]
        o = jnp.concatenate(outs[h], axis=0)
        o = o * lax.rsqrt(jnp.mean(o * o, axis=-1, keepdims=True) + NORM_EPS) * gng_ref[...]
        br_ref[2, :, hs] = (o * _silu(cz_ref[:, hs])).astype(BF16)
    cq_ext[0:CONV_TAIL, :] = cq_ext[TILE:, :]


def _merge(final, x_ref, mg_ref, wbr_ref, wout_ref, fing_ref, br_ref, merged_ref, o_ref):
    half = D_MODEL // 2
    for c0 in range(0, D_MODEL, half):
        acc = None
        for n in range(3):
            up = jnp.dot(br_ref[n], wbr_ref[n, :, c0:c0 + half], preferred_element_type=F32)
            term = _sigmoid(mg_ref[:, n * D_MODEL + c0:n * D_MODEL + c0 + half]) * up
            acc = term if acc is None else acc + term
        merged_ref[:, c0:c0 + half] = acc.astype(BF16)
    y = x_ref[...] + jnp.dot(merged_ref[...], wout_ref[...], preferred_element_type=F32)
    if final:
        y = y * lax.rsqrt(jnp.mean(y * y, axis=-1, keepdims=True) + NORM_EPS) * fing_ref[...]
    o_ref[...] = y


def _layer_kernel(final, tiles_per_seq, x_ref, ng_ref, *refs):
    n_sec = len(_SECTIONS)
    names = [name for name, _ in _SECTIONS]
    w_refs = dict(zip(names, refs[:n_sec]))
    (poolw_ref, pscale_ref, sinks_ref, convw_ref, alog_ref, dtb_ref, gng_ref, wbr_ref, wout_ref, fing_ref,
     o_ref) = refs[n_sec:n_sec + 11]
    scratch = refs[n_sec + 11:]
    sec = dict(zip(names, scratch[:n_sec]))
    (pv_tail, kprev, vprev, state, hbuf, br) = scratch[n_sec:]
    merged = hbuf
    cq_ext = sec["cqkv"]
    dst = dict(sec, cqkv=cq_ext.at[pl.ds(CONV_TAIL, TILE)])

    def tile_body(t, carry):
        row0 = pl.multiple_of(t * TILE, TILE)
        x_tile = x_ref.at[pl.ds(row0, TILE)]
        s_idx = (STEP_TILES * pl.program_id(0) + t) % tiles_per_seq

        @pl.when(s_idx == 0)
        def _():
            pv_tail[...] = jnp.zeros_like(pv_tail)
            cq_ext[0:CONV_TAIL, :] = jnp.zeros((CONV_TAIL, 3 * BRANCH_WIDTH), F32)
            kprev[...] = jnp.zeros_like(kprev)
            vprev[...] = jnp.zeros_like(vprev)
            state[...] = jnp.zeros_like(state)

        _normalise(x_tile, ng_ref, hbuf)
        _project(("pv", "pz", "aq", "az", "akv", "cqkv", "cab"), w_refs, hbuf, dst)
        _pool_branch(s_idx, sec["pv"], sec["pz"], poolw_ref, pscale_ref, pv_tail, br)
        _attn_branch(s_idx == 0, sec["aq"], sec["akv"], sec["az"], sinks_ref, kprev, vprev, br)
        _project(("cz", "mg"), w_refs, hbuf, dst)
        _gdn_branch(cq_ext, sec["cab"], sec["cz"], convw_ref, alog_ref, dtb_ref, gng_ref, state, br)
        _merge(final, x_tile, sec["mg"], wbr_ref, wout_ref, fing_ref, br, merged, o_ref.at[pl.ds(row0, TILE)])
        return carry

    lax.fori_loop(0, STEP_TILES, tile_body, 0)


def _layer(xt, ng_row, w_secs, poolw, pscale, sinks, convw, alog_col, dtb_col, gng_row, wbr, wout,
           fing_row, batch, final):
    t = xt.shape[0]
    n_tiles = t // TILE
    tiles_per_seq = n_tiles // batch
    assert n_tiles % STEP_TILES == 0
    const2 = lambda i: (0, 0)
    const3 = lambda i: (0, 0, 0)
    once = pl.Buffered(1)
    in_specs = [pl.BlockSpec((STEP_TILES * TILE, D_MODEL), lambda i: (i, 0)),
                pl.BlockSpec(ng_row.shape, const2)]
    in_specs += [pl.BlockSpec(w.shape, const2, pipeline_mode=once) for w in w_secs]
    in_specs += [
        pl.BlockSpec(poolw.shape, const3),
        pl.BlockSpec(pscale.shape, const2),
        pl.BlockSpec(memory_space=pltpu.SMEM),
        pl.BlockSpec(convw.shape, const2),
        pl.BlockSpec(alog_col.shape, const2),
        pl.BlockSpec(dtb_col.shape, const2),
        pl.BlockSpec(gng_row.shape, const2),
        pl.BlockSpec(wbr.shape, const3, pipeline_mode=once),
        pl.BlockSpec(wout.shape, const2, pipeline_mode=once),
        pl.BlockSpec(fing_row.shape, const2),
    ]
    sec_scratch = [pltpu.VMEM((TILE + (CONV_TAIL if name == "cqkv" else 0), w), F32) for name, w in _SECTIONS]
    return pl.pallas_call(
        functools.partial(_layer_kernel, final, tiles_per_seq),
        grid=(n_tiles // STEP_TILES,),
        in_specs=in_specs,
        out_specs=pl.BlockSpec((STEP_TILES * TILE, D_MODEL), lambda i: (i, 0)),
        out_shape=jax.ShapeDtypeStruct((t, D_MODEL), F32),
        scratch_shapes=sec_scratch + [
            pltpu.VMEM((POOL_TAIL, BRANCH_WIDTH), F32),
            pltpu.VMEM((ATTN_WINDOW, LANES), F32),
            pltpu.VMEM((ATTN_WINDOW, LANES), F32),
            pltpu.VMEM((GDN_HEADS, GDN_HEAD_DIM, GDN_HEAD_DIM), F32),
            pltpu.VMEM((TILE, D_MODEL), BF16),
            pltpu.VMEM((3, TILE, BRANCH_WIDTH), BF16),
        ],
        compiler_params=pltpu.CompilerParams(dimension_semantics=("arbitrary",),
                                             vmem_limit_bytes=VMEM_LIMIT),
        name="layer_final" if final else "layer",
    )(xt, ng_row, *w_secs, poolw, pscale, sinks, convw, alog_col, dtb_col, gng_row, wbr, wout, fing_row)


def _w_in_sections(w):
    parts = []
    for name, width in _SECTIONS:
        lo, hi = _COLS[name]
        part = w[:, lo:hi].astype(BF16)
        if hi - lo < width:
            part = jnp.pad(part, ((0, 0), (0, width - (hi - lo))))
        parts.append(part)
    return parts


def _head_col(v):
    return jnp.pad(v.astype(F32), (0, 8 - v.shape[0])).reshape(8, 1)


def _pool_pairs(w):
    z = jnp.zeros_like(w[0])
    return jnp.stack([jnp.block([[w[2 * p], z], [z, w[2 * p + 1]]]) for p in range(len(POOL_WINDOWS) // 2)]
                     ).astype(BF16)


def kernel(x, norm_g, w_in, pool_w, pool_scale, attn_sinks, conv_w, a_log, dt_bias, gdn_norm_g, w_branch,
           w_out, final_norm_g):
    batch, seq, d = x.shape
    assert d == D_MODEL and seq % TILE == 0
    xt = x.reshape(batch * seq, d)
    for l in range(DEPTH):
        xt = _layer(xt, norm_g[l].reshape(1, d), _w_in_sections(w_in[l]),
                    _pool_pairs(pool_w[l]), pool_scale[l].reshape(1, BRANCH_WIDTH), attn_sinks[l],
                    conv_w[l], _head_col(a_log[l]), _head_col(dt_bias[l]),
                    gdn_norm_g[l].reshape(1, GDN_HEAD_DIM),
                    w_branch[l].astype(BF16), w_out[l].astype(BF16), final_norm_g.reshape(1, d),
                    batch, final=(l == DEPTH - 1))
    return xt.reshape(batch, seq, d)
```

```python
import functools

import jax
import jax.numpy as jnp
from jax import lax
from jax.experimental import pallas as pl
from jax.experimental.pallas import tpu as pltpu

F32 = jnp.float32
BF16 = jnp.bfloat16

D_MODEL = 1024
DEPTH = 2
BRANCH_WIDTH = 512
POOL_WINDOWS = (2, 4, 8, 16)
POOL_CH = 128
ATTN_HEAD_DIM = 64
ATTN_HEADS = 8
ATTN_GROUP = 4
ATTN_WINDOW = 128
GDN_HEAD_DIM = 128
GDN_HEADS = 4
GDN_CONV = 4
GDN_BLOCK = 128
NORM_EPS = 1e-6
LOG2E = 1.4426950408889634

_COLS = dict(pv=(0, 512), pz=(512, 1024), aq=(1024, 1536), akv=(1536, 1792),
             az=(1792, 2304), cqkv=(2304, 3840), cab=(3840, 3848), cz=(3848, 4360), mg=(4360, 7432))
_SECTIONS = (("mg", 3072), ("cqkv", 1536), ("cz", 512), ("cab", 128), ("pv", 512), ("pz", 512),
             ("aq", 512), ("az", 512), ("akv", 256))

LANES = 128
TILE = 256
STEP_TILES = 4
PROJ_CHUNK = 512
POOL_TAIL = 16
CONV_TAIL = 8
VMEM_LIMIT = 56 * 1024 * 1024


def _sigmoid(v):
    return 0.5 * jnp.tanh(0.5 * v) + 0.5


def _silu(v):
    hv = 0.5 * v
    return hv * (jnp.tanh(hv) + 1.0)


def _mm(a, b):
    return jnp.dot(a.astype(BF16), b.astype(BF16), preferred_element_type=F32)


def _mm_tb(a, b):
    return lax.dot_general(a.astype(BF16), b.astype(BF16), (((1,), (1,)), ((), ())),
                           preferred_element_type=F32)


def _normalise(x_ref, g_ref, h_ref):
    x = x_ref[...]
    ms = jnp.mean(x * x, axis=-1, keepdims=True)
    h_ref[...] = (x * lax.rsqrt(ms + NORM_EPS) * g_ref[...]).astype(BF16)


def _project(names, w_refs, h_ref, dst):
    widths = dict(_SECTIONS)
    for name in names:
        for c in range(0, widths[name], PROJ_CHUNK):
            cw = min(PROJ_CHUNK, widths[name] - c)
            dst[name][:, c:c + cw] = jnp.dot(h_ref[...], w_refs[name][:, c:c + cw],
                                             preferred_element_type=F32)


def _pool_branch(s_idx, pv_ref, pz_ref, poolw_ref, pscale_ref, pv_tail, br_ref):
    pos = (s_idx * TILE + 1 + lax.broadcasted_iota(jnp.int32, (TILE, LANES), 0)).astype(F32)
    pooled = []
    for gi, win in enumerate(POOL_WINDOWS):
        sl = slice(gi * POOL_CH, (gi + 1) * POOL_CH)
        cur = pv_ref[:, sl]
        acc = jnp.concatenate([pv_tail[:, sl], cur], axis=0)
        k = 1
        while k < win:
            acc = acc + pltpu.roll(acc, k, 0)
            k *= 2
        pooled.append((acc[POOL_TAIL:] / jnp.minimum(pos, float(win)) - cur).astype(BF16))
        pv_tail[:, sl] = cur[TILE - POOL_TAIL:]
        if gi % 2 == 1:
            sl2 = slice((gi - 1) * POOL_CH, (gi + 1) * POOL_CH)
            mixed = _mm(jnp.concatenate(pooled[-2:], axis=1), poolw_ref[gi // 2]) * pscale_ref[:, sl2]
            br_ref[0, :, sl2] = (mixed * _silu(pz_ref[:, sl2])).astype(BF16)


def _attn_branch(first_in_seq, aq_ref, akv_ref, az_ref, sinks_ref, kprev, vprev, br_ref):
    w = ATTN_WINDOW
    k_all = jnp.concatenate([kprev[...], akv_ref[:, :LANES]], axis=0)
    v_all = jnp.concatenate([vprev[...], akv_ref[:, LANES:]], axis=0)
    kprev[...] = akv_ref[TILE - w:, :LANES]
    vprev[...] = akv_ref[TILE - w:, LANES:]
    k_sw = pltpu.roll(k_all, ATTN_HEAD_DIM, 1).astype(BF16)
    v_sw = pltpu.roll(v_all, ATTN_HEAD_DIM, 1).astype(BF16)
    k_all = k_all.astype(BF16)
    v_all = v_all.astype(BF16)

    qi = lax.broadcasted_iota(jnp.int32, (w, 2 * w), 0)
    kj = lax.broadcasted_iota(jnp.int32, (w, 2 * w), 1)
    dist = qi + w - kj
    in_window = (dist >= 0) & (dist < w)
    distf = dist.astype(F32)
    lane_lo = lax.broadcasted_iota(jnp.int32, (w, LANES), 1) < ATTN_HEAD_DIM
    scale = ATTN_HEAD_DIM ** -0.5 * LOG2E

    for blk in range(TILE // w):
        rows = slice(blk * w, (blk + 1) * w)
        keys = slice(blk * w, blk * w + 2 * w)
        valid = in_window
        if blk == 0:
            valid = valid & ((kj >= w) | jnp.logical_not(first_in_seq))
        res = [None] * ATTN_HEADS
        for same in (True, False):
            heads = [h for h in range(ATTN_HEADS) if ((h // ATTN_GROUP) == (h % 2)) == same]
            k_op = (k_all if same else k_sw)[keys]
            v_op = (v_all if same else v_sw)[keys]
            q_stack = jnp.concatenate(
                [jnp.where(lane_lo if h % 2 == 0 else ~lane_lo,
                           aq_ref[rows, (h // 2) * LANES:(h // 2 + 1) * LANES] * scale, 0.0).astype(BF16)
                 for h in heads], axis=0)
            sc_all = _mm_tb(q_stack, k_op)
            probs, denoms = [], []
            for i, h in enumerate(heads):
                slope = 2.0 ** (-(h + 1)) * LOG2E
                sc = jnp.where(valid, sc_all[i * w:(i + 1) * w] - slope * distf, -jnp.inf)
                sink = sinks_ref[h] * LOG2E
                m = jnp.maximum(jnp.max(sc, axis=-1, keepdims=True), sink)
                p = jnp.exp2(sc - m)
                denoms.append(jnp.sum(p, axis=-1, keepdims=True) + jnp.exp2(sink - m))
                probs.append(p.astype(BF16))
            o_all = _mm(jnp.concatenate(probs, axis=0), v_op)
            for i, h in enumerate(heads):
                res[h] = o_all[i * w:(i + 1) * w] / denoms[i]
        for pair in range(ATTN_HEADS // 2):
            cols = slice(pair * LANES, (pair + 1) * LANES)
            o_pair = jnp.where(lane_lo, res[2 * pair], res[2 * pair + 1])
            br_ref[1, rows, cols] = (o_pair * _silu(az_ref[rows, cols])).astype(BF16)


def _conv_silu(cq_ext, convw_ref, slab):
    sl = slice(slab * LANES, (slab + 1) * LANES)
    acc = None
    for back in range(GDN_CONV):
        tap = GDN_CONV - 1 - back
        term = cq_ext[CONV_TAIL - back:CONV_TAIL - back + TILE, sl] * convw_ref[tap:tap + 1, sl]
        acc = term if acc is None else acc + term
    return _silu(acc)


def _l2_norm(v):
    return v * lax.rsqrt(jnp.sum(v * v, axis=-1, keepdims=True) + NORM_EPS)


def _gdn_branch(cq_ext, cab_ref, cz_ref, convw_ref, alog_ref, dtb_ref, gng_ref, state, br_ref):
    c = GDN_BLOCK
    n_chunks = TILE // c
    heads = range(GDN_HEADS)
    x8 = cab_ref[...].T[0:8]
    sp_in = x8 + dtb_ref[...]
    softplus = jnp.maximum(sp_in, 0.0) + jnp.log(1.0 + jnp.exp(-jnp.abs(sp_in)))
    gc_t = (-LOG2E) * jnp.exp(alog_ref[...]) * softplus
    lane_in_chunk = lax.broadcasted_iota(jnp.int32, (8, TILE), 1) & (c - 1)
    k = 1
    while k < c:
        gc_t = gc_t + jnp.where(lane_in_chunk >= k, pltpu.roll(gc_t, k, 1), 0.0)
        k *= 2
    gl_t = jnp.concatenate([jnp.broadcast_to(gc_t[:, ch * c + c - 1:(ch + 1) * c], (8, c))
                            for ch in range(n_chunks)], axis=1)
    stack = jnp.concatenate([gc_t, _sigmoid(x8), jnp.exp2(gc_t), jnp.exp2(gl_t - gc_t), jnp.exp2(gl_t),
                             jnp.zeros((LANES - 40, TILE), F32)], axis=0)
    cols = stack.T
    gc_all, beta_all, eg_all, ekd_all, egl_all = (cols[:, 8 * i:8 * i + 8] for i in range(5))

    ri = lax.broadcasted_iota(jnp.int32, (TILE, TILE), 0)
    ci = lax.broadcasted_iota(jnp.int32, (TILE, TILE), 1)
    same_chunk = (ri & -c) == (ci & -c)
    causal = same_chunk & (ri >= ci)
    strict = same_chunk & (ri > ci)
    eye = (ri == ci).astype(F32)
    scale = GDN_HEAD_DIM ** -0.5

    a, qk, rhs, qd, kd_t = [], [], [], [], []
    for h in heads:
        qn = _l2_norm(_conv_silu(cq_ext, convw_ref, h)) * scale
        kn = _l2_norm(_conv_silu(cq_ext, convw_ref, GDN_HEADS + h))
        v = _conv_silu(cq_ext, convw_ref, 2 * GDN_HEADS + h)
        beta = beta_all[:, GDN_HEADS + h:GDN_HEADS + h + 1]
        eg = eg_all[:, h:h + 1]
        kb = kn * beta
        prod = _mm_tb(jnp.concatenate([kb, qn], axis=0), kn)
        decay = jnp.exp2(jnp.where(causal, gc_all[:, h:h + 1] - gc_t[h:h + 1, :], -jnp.inf))
        a.append(jnp.where(strict, prod[:TILE] * decay, 0.0).astype(BF16))
        qk.append((prod[TILE:] * decay).astype(BF16))
        rhs.append(jnp.concatenate([v * beta, kb * eg], axis=1).astype(BF16))
        qd.append(qn * eg)
        kd_t.append((kn * ekd_all[:, h:h + 1]).T.astype(BF16))

    rc = ri ^ ci
    zero = jnp.zeros((), BF16)
    inv = [eye - jnp.where(rc == 1, a_h, zero) for a_h in a]
    s = 2
    while s < c:
        link = (rc & -s) == s
        lnk = [jnp.where(link, a_h, zero) for a_h in a]
        if s < 8:
            tl = [_mm(i, l) for i, l in zip(inv, lnk)]
            inv = [i - _mm(t, i) for i, t in zip(inv, tl)]
        else:
            lower = [slice(b + s, b + 2 * s) for b in range(0, TILE, 2 * s)]
            upper = [slice(b, b + s) for b in range(0, TILE, 2 * s)]
            tl = [_mm(jnp.concatenate([i[r] for r in lower], axis=0), l) for i, l in zip(inv, lnk)]
            upd = [_mm(t, i) for i, t in zip(inv, tl)]
            inv = [jnp.concatenate([piece for n, (u, lo) in enumerate(zip(upper, lower))
                                    for piece in (i[u], i[lo] - d[n * s:(n + 1) * s])], axis=0)
                   for i, d in zip(inv, upd)]
        s *= 2
    uw = [_mm(i, r) for i, r in zip(inv, rhs)]

    eye_c = lax.broadcasted_iota(jnp.int32, (c, c), 0) == lax.broadcasted_iota(jnp.int32, (c, c), 1)
    lhs = [[None] * n_chunks for _ in heads]
    add = [[None] * n_chunks for _ in heads]
    for ch in range(n_chunks):
        rows = slice(ch * c, (ch + 1) * c)
        for h in heads:
            uw_c = uw[h][rows].astype(BF16)
            k_uw = _mm(kd_t[h][:, rows], uw_c)
            q_uw = _mm(qk[h][rows, rows], uw_c)
            dec = jnp.broadcast_to(egl_all[rows, h:h + 1], (c, LANES))
            lhs[h][ch] = jnp.concatenate([jnp.where(eye_c, dec, 0.0) - k_uw[:, LANES:],
                                          qd[h][rows] - q_uw[:, LANES:]], axis=0).astype(BF16)
            add[h][ch] = (k_uw[:, :LANES], q_uw[:, :LANES])
    s_cur = [state[h] for h in heads]
    outs = [[] for _ in heads]
    for ch in range(n_chunks):
        for h in heads:
            prod_s = _mm(lhs[h][ch], s_cur[h])
            outs[h].append(prod_s[c:] + add[h][ch][1])
            s_cur[h] = prod_s[:c] + add[h][ch][0]
    for h in heads:
        hs = slice(h * LANES, (h + 1) * LANES)
        state[h] = s_cur[h]
        o = jnp.concatenate(outs[h], axis=0)
        o = o * lax.rsqrt(jnp.mean(o * o, axis=-1, keepdims=True) + NORM_EPS) * gng_ref[...]
        br_ref[2, :, hs] = (o * _silu(cz_ref[:, hs])).astype(BF16)
    cq_ext[0:CONV_TAIL, :] = cq_ext[TILE:, :]


def _merge(final, x_ref, mg_ref, wbr_ref, wout_ref, fing_ref, br_ref, merged_ref, o_ref):
    half = D_MODEL // 2
    for c0 in range(0, D_MODEL, half):
        acc = None
        for n in range(3):
            up = jnp.dot(br_ref[n], wbr_ref[n, :, c0:c0 + half], preferred_element_type=F32)
            term = _sigmoid(mg_ref[:, n * D_MODEL + c0:n * D_MODEL + c0 + half]) * up
            acc = term if acc is None else acc + term
        merged_ref[:, c0:c0 + half] = acc.astype(BF16)
    y = x_ref[...] + jnp.dot(merged_ref[...], wout_ref[...], preferred_element_type=F32)
    if final:
        y = y * lax.rsqrt(jnp.mean(y * y, axis=-1, keepdims=True) + NORM_EPS) * fing_ref[...]
    o_ref[...] = y


def _layer_kernel(final, tiles_per_seq, x_ref, ng_ref, *refs):
    n_sec = len(_SECTIONS)
    names = [name for name, _ in _SECTIONS]
    w_refs = dict(zip(names, refs[:n_sec]))
    (poolw_ref, pscale_ref, sinks_ref, convw_ref, alog_ref, dtb_ref, gng_ref, wbr_ref, wout_ref, fing_ref,
     o_ref) = refs[n_sec:n_sec + 11]
    scratch = refs[n_sec + 11:]
    sec = dict(zip(names, scratch[:n_sec]))
    (pv_tail, kprev, vprev, state, hbuf, br) = scratch[n_sec:]
    merged = hbuf
    cq_ext = sec["cqkv"]
    dst = dict(sec, cqkv=cq_ext.at[pl.ds(CONV_TAIL, TILE)])

    def tile_body(t, carry):
        row0 = pl.multiple_of(t * TILE, TILE)
        x_tile = x_ref.at[pl.ds(row0, TILE)]
        s_idx = (STEP_TILES * pl.program_id(0) + t) % tiles_per_seq

        @pl.when(s_idx == 0)
        def _():
            pv_tail[...] = jnp.zeros_like(pv_tail)
            cq_ext[0:CONV_TAIL, :] = jnp.zeros((CONV_TAIL, 3 * BRANCH_WIDTH), F32)
            kprev[...] = jnp.zeros_like(kprev)
            vprev[...] = jnp.zeros_like(vprev)
            state[...] = jnp.zeros_like(state)

        _normalise(x_tile, ng_ref, hbuf)
        _project(("pv", "pz", "aq", "az", "akv", "cqkv", "cab"), w_refs, hbuf, dst)
        _pool_branch(s_idx, sec["pv"], sec["pz"], poolw_ref, pscale_ref, pv_tail, br)
        _attn_branch(s_idx == 0, sec["aq"], sec["akv"], sec["az"], sinks_ref, kprev, vprev, br)
        _project(("cz", "mg"), w_refs, hbuf, dst)
        _gdn_branch(cq_ext, sec["cab"], sec["cz"], convw_ref, alog_ref, dtb_ref, gng_ref, state, br)
        _merge(final, x_tile, sec["mg"], wbr_ref, wout_ref, fing_ref, br, merged, o_ref.at[pl.ds(row0, TILE)])
        return carry

    lax.fori_loop(0, STEP_TILES, tile_body, 0)


def _layer(xt, ng_row, w_secs, poolw, pscale, sinks, convw, alog_col, dtb_col, gng_row, wbr, wout,
           fing_row, batch, final):
    t = xt.shape[0]
    n_tiles = t // TILE
    tiles_per_seq = n_tiles // batch
    assert n_tiles % STEP_TILES == 0
    const2 = lambda i: (0, 0)
    const3 = lambda i: (0, 0, 0)
    once = pl.Buffered(1)
    in_specs = [pl.BlockSpec((STEP_TILES * TILE, D_MODEL), lambda i: (i, 0)),
                pl.BlockSpec(ng_row.shape, const2)]
    in_specs += [pl.BlockSpec(w.shape, const2, pipeline_mode=once) for w in w_secs]
    in_specs += [
        pl.BlockSpec(poolw.shape, const3),
        pl.BlockSpec(pscale.shape, const2),
        pl.BlockSpec(memory_space=pltpu.SMEM),
        pl.BlockSpec(convw.shape, const2),
        pl.BlockSpec(alog_col.shape, const2),
        pl.BlockSpec(dtb_col.shape, const2),
        pl.BlockSpec(gng_row.shape, const2),
        pl.BlockSpec(wbr.shape, const3, pipeline_mode=once),
        pl.BlockSpec(wout.shape, const2, pipeline_mode=once),
        pl.BlockSpec(fing_row.shape, const2),
    ]
    sec_scratch = [pltpu.VMEM((TILE + (CONV_TAIL if name == "cqkv" else 0), w), F32) for name, w in _SECTIONS]
    return pl.pallas_call(
        functools.partial(_layer_kernel, final, tiles_per_seq),
        grid=(n_tiles // STEP_TILES,),
        in_specs=in_specs,
        out_specs=pl.BlockSpec((STEP_TILES * TILE, D_MODEL), lambda i: (i, 0)),
        out_shape=jax.ShapeDtypeStruct((t, D_MODEL), F32),
        scratch_shapes=sec_scratch + [
            pltpu.VMEM((POOL_TAIL, BRANCH_WIDTH), F32),
            pltpu.VMEM((ATTN_WINDOW, LANES), F32),
            pltpu.VMEM((ATTN_WINDOW, LANES), F32),
            pltpu.VMEM((GDN_HEADS, GDN_HEAD_DIM, GDN_HEAD_DIM), F32),
            pltpu.VMEM((TILE, D_MODEL), BF16),
            pltpu.VMEM((3, TILE, BRANCH_WIDTH), BF16),
        ],
        compiler_params=pltpu.CompilerParams(dimension_semantics=("arbitrary",),
                                             vmem_limit_bytes=VMEM_LIMIT),
        name="layer_final" if final else "layer",
    )(xt, ng_row, *w_secs, poolw, pscale, sinks, convw, alog_col, dtb_col, gng_row, wbr, wout, fing_row)


def _w_in_sections(w):
    parts = []
    for name, width in _SECTIONS:
        lo, hi = _COLS[name]
        part = w[:, lo:hi].astype(BF16)
        if hi - lo < width:
            part = jnp.pad(part, ((0, 0), (0, width - (hi - lo))))
        parts.append(part)
    return parts


def _head_col(v):
    return jnp.pad(v.astype(F32), (0, 8 - v.shape[0])).reshape(8, 1)


def _pool_pairs(w):
    z = jnp.zeros_like(w[0])
    return jnp.stack([jnp.block([[w[2 * p], z], [z, w[2 * p + 1]]]) for p in range(len(POOL_WINDOWS) // 2)]
                     ).astype(BF16)


def kernel(x, norm_g, w_in, pool_w, pool_scale, attn_sinks, conv_w, a_log, dt_bias, gdn_norm_g, w_branch,
           w_out, final_norm_g):
    batch, seq, d = x.shape
    assert d == D_MODEL and seq % TILE == 0
    xt = x.reshape(batch * seq, d)
    for l in range(DEPTH):
        xt = _layer(xt, norm_g[l].reshape(1, d), _w_in_sections(w_in[l]),
                    _pool_pairs(pool_w[l]), pool_scale[l].reshape(1, BRANCH_WIDTH), attn_sinks[l],
                    conv_w[l], _head_col(a_log[l]), _head_col(dt_bias[l]),
                    gdn_norm_g[l].reshape(1, GDN_HEAD_DIM),
                    w_branch[l].astype(BF16), w_out[l].astype(BF16), final_norm_g.reshape(1, d),
                    batch, final=(l == DEPTH - 1))
    return xt.reshape(batch, seq, d)
```

```python
import functools

import jax
import jax.numpy as jnp
from jax import lax
from jax.experimental import pallas as pl
from jax.experimental.pallas import tpu as pltpu

F32 = jnp.float32
BF16 = jnp.bfloat16

D_MODEL = 1024
DEPTH = 2
BRANCH_WIDTH = 512
POOL_WINDOWS = (2, 4, 8, 16)
POOL_CH = 128
ATTN_HEAD_DIM = 64
ATTN_HEADS = 8
ATTN_GROUP = 4
ATTN_WINDOW = 128
GDN_HEAD_DIM = 128
GDN_HEADS = 4
GDN_CONV = 4
GDN_BLOCK = 128
NORM_EPS = 1e-6
LOG2E = 1.4426950408889634

_COLS = dict(pv=(0, 512), pz=(512, 1024), aq=(1024, 1536), akv=(1536, 1792),
             az=(1792, 2304), cqkv=(2304, 3840), cab=(3840, 3848), cz=(3848, 4360), mg=(4360, 7432))
_SECTIONS = (("mg", 3072), ("cqkv", 1536), ("cz", 512), ("cab", 128), ("pv", 512), ("pz", 512),
             ("aq", 512), ("az", 512), ("akv", 256))

LANES = 128
TILE = 256
STEP_TILES = 2
PROJ_CHUNK = 512
POOL_TAIL = 16
CONV_TAIL = 8
VMEM_LIMIT = 56 * 1024 * 1024


def _sigmoid(v):
    return 0.5 * jnp.tanh(0.5 * v) + 0.5


def _silu(v):
    hv = 0.5 * v
    return hv * (jnp.tanh(hv) + 1.0)


def _mm(a, b):
    return jnp.dot(a.astype(BF16), b.astype(BF16), preferred_element_type=F32)


def _mm_tb(a, b):
    return lax.dot_general(a.astype(BF16), b.astype(BF16), (((1,), (1,)), ((), ())),
                           preferred_element_type=F32)


def _normalise(x_ref, g_ref, h_ref):
    x = x_ref[...]
    ms = jnp.mean(x * x, axis=-1, keepdims=True)
    h_ref[...] = (x * lax.rsqrt(ms + NORM_EPS) * g_ref[...]).astype(BF16)


def _project(names, w_refs, h_ref, dst):
    widths = dict(_SECTIONS)
    for name in names:
        for c in range(0, widths[name], PROJ_CHUNK):
            cw = min(PROJ_CHUNK, widths[name] - c)
            dst[name][:, c:c + cw] = jnp.dot(h_ref[...], w_refs[name][:, c:c + cw],
                                             preferred_element_type=F32)


def _pool_branch(s_idx, pv_ref, pz_ref, poolw_ref, pscale_ref, pv_tail, br_ref):
    pos = (s_idx * TILE + 1 + lax.broadcasted_iota(jnp.int32, (TILE, LANES), 0)).astype(F32)
    pooled = []
    for gi, win in enumerate(POOL_WINDOWS):
        sl = slice(gi * POOL_CH, (gi + 1) * POOL_CH)
        cur = pv_ref[:, sl]
        acc = jnp.concatenate([pv_tail[:, sl], cur], axis=0)
        k = 1
        while k < win:
            acc = acc + pltpu.roll(acc, k, 0)
            k *= 2
        pooled.append((acc[POOL_TAIL:] / jnp.minimum(pos, float(win)) - cur).astype(BF16))
        pv_tail[:, sl] = cur[TILE - POOL_TAIL:]
        if gi % 2 == 1:
            sl2 = slice((gi - 1) * POOL_CH, (gi + 1) * POOL_CH)
            mixed = _mm(jnp.concatenate(pooled[-2:], axis=1), poolw_ref[gi // 2]) * pscale_ref[:, sl2]
            br_ref[0, :, sl2] = (mixed * _silu(pz_ref[:, sl2])).astype(BF16)


def _attn_branch(first_in_seq, aq_ref, akv_ref, az_ref, sinks_ref, kprev, vprev, br_ref):
    w = ATTN_WINDOW
    k_all = jnp.concatenate([kprev[...], akv_ref[:, :LANES]], axis=0)
    v_all = jnp.concatenate([vprev[...], akv_ref[:, LANES:]], axis=0)
    kprev[...] = akv_ref[TILE - w:, :LANES]
    vprev[...] = akv_ref[TILE - w:, LANES:]
    k_sw = pltpu.roll(k_all, ATTN_HEAD_DIM, 1).astype(BF16)
    v_sw = pltpu.roll(v_all, ATTN_HEAD_DIM, 1).astype(BF16)
    k_all = k_all.astype(BF16)
    v_all = v_all.astype(BF16)

    qi = lax.broadcasted_iota(jnp.int32, (w, 2 * w), 0)
    kj = lax.broadcasted_iota(jnp.int32, (w, 2 * w), 1)
    dist = qi + w - kj
    in_window = (dist >= 0) & (dist < w)
    distf = dist.astype(F32)
    lane_lo = lax.broadcasted_iota(jnp.int32, (w, LANES), 1) < ATTN_HEAD_DIM
    scale = ATTN_HEAD_DIM ** -0.5 * LOG2E

    for blk in range(TILE // w):
        rows = slice(blk * w, (blk + 1) * w)
        keys = slice(blk * w, blk * w + 2 * w)
        valid = in_window
        if blk == 0:
            valid = valid & ((kj >= w) | jnp.logical_not(first_in_seq))
        res = [None] * ATTN_HEADS
        for same in (True, False):
            heads = [h for h in range(ATTN_HEADS) if ((h // ATTN_GROUP) == (h % 2)) == same]
            k_op = (k_all if same else k_sw)[keys]
            v_op = (v_all if same else v_sw)[keys]
            q_stack = jnp.concatenate(
                [jnp.where(lane_lo if h % 2 == 0 else ~lane_lo,
                           aq_ref[rows, (h // 2) * LANES:(h // 2 + 1) * LANES] * scale, 0.0).astype(BF16)
                 for h in heads], axis=0)
            sc_all = _mm_tb(q_stack, k_op)
            probs, denoms = [], []
            for i, h in enumerate(heads):
                slope = 2.0 ** (-(h + 1)) * LOG2E
                sc = jnp.where(valid, sc_all[i * w:(i + 1) * w] - slope * distf, -jnp.inf)
                sink = sinks_ref[h] * LOG2E
                m = jnp.maximum(jnp.max(sc, axis=-1, keepdims=True), sink)
                p = jnp.exp2(sc - m)
                denoms.append(jnp.sum(p, axis=-1, keepdims=True) + jnp.exp2(sink - m))
                probs.append(p.astype(BF16))
            o_all = _mm(jnp.concatenate(probs, axis=0), v_op)
            for i, h in enumerate(heads):
                res[h] = o_all[i * w:(i + 1) * w] / denoms[i]
        for pair in range(ATTN_HEADS // 2):
            cols = slice(pair * LANES, (pair + 1) * LANES)
            o_pair = jnp.where(lane_lo, res[2 * pair], res[2 * pair + 1])
            br_ref[1, rows, cols] = (o_pair * _silu(az_ref[rows, cols])).astype(BF16)


def _conv_silu(cq_ext, convw_ref, slab):
    sl = slice(slab * LANES, (slab + 1) * LANES)
    acc = None
    for back in range(GDN_CONV):
        tap = GDN_CONV - 1 - back
        term = cq_ext[CONV_TAIL - back:CONV_TAIL - back + TILE, sl] * convw_ref[tap:tap + 1, sl]
        acc = term if acc is None else acc + term
    return _silu(acc)


def _l2_norm(v):
    return v * lax.rsqrt(jnp.sum(v * v, axis=-1, keepdims=True) + NORM_EPS)


def _gdn_branch(cq_ext, cab_ref, cz_ref, convw_ref, alog_ref, dtb_ref, gng_ref, state, br_ref):
    c = GDN_BLOCK
    n_chunks = TILE // c
    heads = range(GDN_HEADS)
    x8 = cab_ref[...].T[0:8]
    sp_in = x8 + dtb_ref[...]
    softplus = jnp.maximum(sp_in, 0.0) + jnp.log(1.0 + jnp.exp(-jnp.abs(sp_in)))
    gc_t = (-LOG2E) * jnp.exp(alog_ref[...]) * softplus
    lane_in_chunk = lax.broadcasted_iota(jnp.int32, (8, TILE), 1) & (c - 1)
    k = 1
    while k < c:
        gc_t = gc_t + jnp.where(lane_in_chunk >= k, pltpu.roll(gc_t, k, 1), 0.0)
        k *= 2
    gl_t = jnp.concatenate([jnp.broadcast_to(gc_t[:, ch * c + c - 1:(ch + 1) * c], (8, c))
                            for ch in range(n_chunks)], axis=1)
    stack = jnp.concatenate([gc_t, _sigmoid(x8), jnp.exp2(gc_t), jnp.exp2(gl_t - gc_t), jnp.exp2(gl_t),
                             jnp.zeros((LANES - 40, TILE), F32)], axis=0)
    cols = stack.T
    gc_all, beta_all, eg_all, ekd_all, egl_all = (cols[:, 8 * i:8 * i + 8] for i in range(5))

    ri = lax.broadcasted_iota(jnp.int32, (TILE, TILE), 0)
    ci = lax.broadcasted_iota(jnp.int32, (TILE, TILE), 1)
    same_chunk = (ri & -c) == (ci & -c)
    causal = same_chunk & (ri >= ci)
    strict = same_chunk & (ri > ci)
    eye = (ri == ci).astype(F32)
    scale = GDN_HEAD_DIM ** -0.5

    a, qk, rhs, qd, kd_t = [], [], [], [], []
    for h in heads:
        qn = _l2_norm(_conv_silu(cq_ext, convw_ref, h)) * scale
        kn = _l2_norm(_conv_silu(cq_ext, convw_ref, GDN_HEADS + h))
        v = _conv_silu(cq_ext, convw_ref, 2 * GDN_HEADS + h)
        beta = beta_all[:, GDN_HEADS + h:GDN_HEADS + h + 1]
        eg = eg_all[:, h:h + 1]
        kb = kn * beta
        prod = _mm_tb(jnp.concatenate([kb, qn], axis=0), kn)
        decay = jnp.exp2(jnp.where(causal, gc_all[:, h:h + 1] - gc_t[h:h + 1, :], -jnp.inf))
        a.append(jnp.where(strict, prod[:TILE] * decay, 0.0).astype(BF16))
        qk.append((prod[TILE:] * decay).astype(BF16))
        rhs.append(jnp.concatenate([v * beta, kb * eg], axis=1).astype(BF16))
        qd.append(qn * eg)
        kd_t.append((kn * ekd_all[:, h:h + 1]).T.astype(BF16))

    rc = ri ^ ci
    zero = jnp.zeros((), BF16)
    inv = [eye - jnp.where(rc == 1, a_h, zero) for a_h in a]
    s = 2
    while s < c:
        link = (rc & -s) == s
        lnk = [jnp.where(link, a_h, zero) for a_h in a]
        if s < 8:
            tl = [_mm(i, l) for i, l in zip(inv, lnk)]
            inv = [i - _mm(t, i) for i, t in zip(inv, tl)]
        else:
            lower = [slice(b + s, b + 2 * s) for b in range(0, TILE, 2 * s)]
            upper = [slice(b, b + s) for b in range(0, TILE, 2 * s)]
            tl = [_mm(jnp.concatenate([i[r] for r in lower], axis=0), l) for i, l in zip(inv, lnk)]
            upd = [_mm(t, i) for i, t in zip(inv, tl)]
            inv = [jnp.concatenate([piece for n, (u, lo) in enumerate(zip(upper, lower))
                                    for piece in (i[u], i[lo] - d[n * s:(n + 1) * s])], axis=0)
                   for i, d in zip(inv, upd)]
        s *= 2
    uw = [_mm(i, r) for i, r in zip(inv, rhs)]

    eye_c = lax.broadcasted_iota(jnp.int32, (c, c), 0) == lax.broadcasted_iota(jnp.int32, (c, c), 1)
    lhs = [[None] * n_chunks for _ in heads]
    add = [[None] * n_chunks for _ in heads]
    for ch in range(n_chunks):
        rows = slice(ch * c, (ch + 1) * c)
        for h in heads:
            uw_c = uw[h][rows].astype(BF16)
            k_uw = _mm(kd_t[h][:, rows], uw_c)
            q_uw = _mm(qk[h][rows, rows], uw_c)
            dec = jnp.broadcast_to(egl_all[rows, h:h + 1], (c, LANES))
            lhs[h][ch] = jnp.concatenate([jnp.where(eye_c, dec, 0.0) - k_uw[:, LANES:],
                                          qd[h][rows] - q_uw[:, LANES:]], axis=0).astype(BF16)
            add[h][ch] = (k_uw[:, :LANES], q_uw[:, :LANES])
    s_cur = [state[h] for h in heads]
    outs = [[] for _ in heads]
    for ch in range(n_chunks):
        for h in heads:
            prod_s = _mm(lhs[h][ch], s_cur[h])
            outs[h].append(prod_s[c:] + add[h][ch][1])
            s_cur[h] = prod_s[:c] + add[h][ch][0]
    for h in heads:
        hs = slice(h * LANES, (h + 1) * LANES)
        state[h] = s_cur[h]
        o = jnp.concatenate(outs[h], axis=0)
        o = o * lax.rsqrt(jnp.mean(o * o, axis=-1, keepdims=True) + NORM_EPS) * gng_ref[...]
        br_ref[2, :, hs] = (o * _silu(cz_ref[:, hs])).astype(BF16)
    cq_ext[0:CONV_TAIL, :] = cq_ext[TILE:, :]


def _merge(final, x_ref, mg_ref, wbr_ref, wout_ref, fing_ref, br_ref, merged_ref, o_ref):
    half = D_MODEL // 2
    for c0 in range(0, D_MODEL, half):
        acc = None
        for n in range(3):
            up = jnp.dot(br_ref[n], wbr_ref[n, :, c0:c0 + half], preferred_element_type=F32)
            term = _sigmoid(mg_ref[:, n * D_MODEL + c0:n * D_MODEL + c0 + half]) * up
            acc = term if acc is None else acc + term
        merged_ref[:, c0:c0 + half] = acc.astype(BF16)
    y = x_ref[...] + jnp.dot(merged_ref[...], wout_ref[...], preferred_element_type=F32)
    if final:
        y = y * lax.rsqrt(jnp.mean(y * y, axis=-1, keepdims=True) + NORM_EPS) * fing_ref[...]
    o_ref[...] = y


def _layer_kernel(final, tiles_per_seq, x_ref, ng_ref, *refs):
    n_sec = len(_SECTIONS)
    names = [name for name, _ in _SECTIONS]
    w_refs = dict(zip(names, refs[:n_sec]))
    (poolw_ref, pscale_ref, sinks_ref, convw_ref, alog_ref, dtb_ref, gng_ref, wbr_ref, wout_ref, fing_ref,
     o_ref) = refs[n_sec:n_sec + 11]
    scratch = refs[n_sec + 11:]
    sec = dict(zip(names, scratch[:n_sec]))
    (pv_tail, kprev, vprev, state, hbuf, br) = scratch[n_sec:]
    merged = hbuf
    cq_ext = sec["cqkv"]
    dst = dict(sec, cqkv=cq_ext.at[pl.ds(CONV_TAIL, TILE)])

    def tile_body(t, carry):
        row0 = pl.multiple_of(t * TILE, TILE)
        x_tile = x_ref.at[pl.ds(row0, TILE)]
        s_idx = (STEP_TILES * pl.program_id(0) + t) % tiles_per_seq

        @pl.when(s_idx == 0)
        def _():
            pv_tail[...] = jnp.zeros_like(pv_tail)
            cq_ext[0:CONV_TAIL, :] = jnp.zeros((CONV_TAIL, 3 * BRANCH_WIDTH), F32)
            kprev[...] = jnp.zeros_like(kprev)
            vprev[...] = jnp.zeros_like(vprev)
            state[...] = jnp.zeros_like(state)

        _normalise(x_tile, ng_ref, hbuf)
        _project(("pv", "pz", "aq", "az", "akv", "cqkv", "cab"), w_refs, hbuf, dst)
        _pool_branch(s_idx, sec["pv"], sec["pz"], poolw_ref, pscale_ref, pv_tail, br)
        _attn_branch(s_idx == 0, sec["aq"], sec["akv"], sec["az"], sinks_ref, kprev, vprev, br)
        _project(("cz", "mg"), w_refs, hbuf, dst)
        _gdn_branch(cq_ext, sec["cab"], sec["cz"], convw_ref, alog_ref, dtb_ref, gng_ref, state, br)
        _merge(final, x_tile, sec["mg"], wbr_ref, wout_ref, fing_ref, br, merged, o_ref.at[pl.ds(row0, TILE)])
        return carry

    lax.fori_loop(0, STEP_TILES, tile_body, 0)


def _layer(xt, ng_row, w_secs, poolw, pscale, sinks, convw, alog_col, dtb_col, gng_row, wbr, wout,
           fing_row, batch, final):
    t = xt.shape[0]
    n_tiles = t // TILE
    tiles_per_seq = n_tiles // batch
    assert n_tiles % STEP_TILES == 0
    const2 = lambda i: (0, 0)
    const3 = lambda i: (0, 0, 0)
    once = pl.Buffered(1)
    in_specs = [pl.BlockSpec((STEP_TILES * TILE, D_MODEL), lambda i: (i, 0)),
                pl.BlockSpec(ng_row.shape, const2)]
    in_specs += [pl.BlockSpec(w.shape, const2, pipeline_mode=once) for w in w_secs]
    in_specs += [
        pl.BlockSpec(poolw.shape, const3),
        pl.BlockSpec(pscale.shape, const2),
        pl.BlockSpec(memory_space=pltpu.SMEM),
        pl.BlockSpec(convw.shape, const2),
        pl.BlockSpec(alog_col.shape, const2),
        pl.BlockSpec(dtb_col.shape, const2),
        pl.BlockSpec(gng_row.shape, const2),
        pl.BlockSpec(wbr.shape, const3, pipeline_mode=once),
        pl.BlockSpec(wout.shape, const2, pipeline_mode=once),
        pl.BlockSpec(fing_row.shape, const2),
    ]
    sec_scratch = [pltpu.VMEM((TILE + (CONV_TAIL if name == "cqkv" else 0), w), F32) for name, w in _SECTIONS]
    return pl.pallas_call(
        functools.partial(_layer_kernel, final, tiles_per_seq),
        grid=(n_tiles // STEP_TILES,),
        in_specs=in_specs,
        out_specs=pl.BlockSpec((STEP_TILES * TILE, D_MODEL), lambda i: (i, 0)),
        out_shape=jax.ShapeDtypeStruct((t, D_MODEL), F32),
        scratch_shapes=sec_scratch + [
            pltpu.VMEM((POOL_TAIL, BRANCH_WIDTH), F32),
            pltpu.VMEM((ATTN_WINDOW, LANES), F32),
            pltpu.VMEM((ATTN_WINDOW, LANES), F32),
            pltpu.VMEM((GDN_HEADS, GDN_HEAD_DIM, GDN_HEAD_DIM), F32),
            pltpu.VMEM((TILE, D_MODEL), BF16),
            pltpu.VMEM((3, TILE, BRANCH_WIDTH), BF16),
        ],
        compiler_params=pltpu.CompilerParams(dimension_semantics=("arbitrary",),
                                             vmem_limit_bytes=VMEM_LIMIT),
        name="layer_final" if final else "layer",
    )(xt, ng_row, *w_secs, poolw, pscale, sinks, convw, alog_col, dtb_col, gng_row, wbr, wout, fing_row)


def _w_in_sections(w):
    parts = []
    for name, width in _SECTIONS:
        lo, hi = _COLS[name]
        part = w[:, lo:hi]
        if hi - lo < width:
            part = jnp.pad(part, ((0, 0), (0, width - (hi - lo))))
        parts.append(part)
    return parts


def _head_col(v):
    return jnp.pad(v.astype(F32), (0, 8 - v.shape[0])).reshape(8, 1)


def _pool_pairs(w):
    z = jnp.zeros_like(w[0])
    return jnp.stack([jnp.block([[w[2 * p], z], [z, w[2 * p + 1]]]) for p in range(len(POOL_WINDOWS) // 2)]
                     ).astype(BF16)


def kernel(x, norm_g, w_in, pool_w, pool_scale, attn_sinks, conv_w, a_log, dt_bias, gdn_norm_g, w_branch,
           w_out, final_norm_g):
    batch, seq, d = x.shape
    assert d == D_MODEL and seq % TILE == 0
    xt = x.reshape(batch * seq, d)
    w_in_bf16 = lax.optimization_barrier(w_in.astype(BF16))
    for l in range(DEPTH):
        xt = _layer(xt, norm_g[l].reshape(1, d), _w_in_sections(w_in_bf16[l]),
                    _pool_pairs(pool_w[l]), pool_scale[l].reshape(1, BRANCH_WIDTH), attn_sinks[l],
                    conv_w[l], _head_col(a_log[l]), _head_col(dt_bias[l]),
                    gdn_norm_g[l].reshape(1, GDN_HEAD_DIM),
                    w_branch[l].astype(BF16), w_out[l].astype(BF16), final_norm_g.reshape(1, d),
                    batch, final=(l == DEPTH - 1))
    return xt.reshape(batch, seq, d)
```
